```python
import jax, jax.numpy as jnp
from jax import lax
import numpy as np

D_MODEL = 4096
BATCH = 4
SEQ = 4096
DEPTH = 4

GRID_W = 64
CTX_LEN = 256
HEAD_DIM = 128
N_Q_HEADS = 16
N_KV_HEADS = 4
GQA_GROUP = N_Q_HEADS // N_KV_HEADS
Q_WIDTH = N_Q_HEADS * HEAD_DIM
KV_WIDTH = N_KV_HEADS * HEAD_DIM
Q_BLOCK = 128
ROPE_THETA = 10000.0
LRU_WIDTH = 2048
LRU_BLOCKS = 16
LRU_BLOCK = LRU_WIDTH // LRU_BLOCKS
LRU_C = 8.0
CONV_W = 4
CONV_LEFT = 2
N_EXPERTS = 16
CAP_FACTOR = 2
D_EXPERT = 512
ADA_RANK = 256
ALPHA = (2.0 * DEPTH) ** 0.25
BETA = (8.0 * DEPTH) ** -0.25
IN_WIDTH = 2 * LRU_WIDTH + Q_WIDTH + 2 * KV_WIDTH + 2 * D_MODEL
SPLIT_AT = (LRU_WIDTH, 2 * LRU_WIDTH, 2 * LRU_WIDTH + Q_WIDTH, 2 * LRU_WIDTH + Q_WIDTH + KV_WIDTH, 2 * LRU_WIDTH + Q_WIDTH + 2 * KV_WIDTH, 2 * LRU_WIDTH + Q_WIDTH + 2 * KV_WIDTH + D_MODEL)

kernel_name = "hybrid_rglru_gqa_ec_moe_dit"


def layer_norm(x, g, b, eps=1e-6):
    xf = x.astype(jnp.float32)
    mu = jnp.mean(xf, axis=-1, keepdims=True)
    var = jnp.mean(jnp.square(xf - mu), axis=-1, keepdims=True)
    return ((xf - mu) * lax.rsqrt(var + eps) * g.astype(jnp.float32) + b.astype(jnp.float32)).astype(x.dtype)


def rms_norm(x, g, eps=1e-6):
    xf = x.astype(jnp.float32)
    return (xf * lax.rsqrt(jnp.mean(xf * xf, axis=-1, keepdims=True) + eps) * g.astype(jnp.float32)).astype(x.dtype)


def modulate(h, shift, scale):
    return h * (1 + scale) + shift


def ada_modulation(cvec, w_dn, w_up, b_up):
    m = (jax.nn.silu(cvec) @ w_dn) @ w_up + b_up
    m = m.reshape(m.shape[:-1] + (1, 6 * D_MODEL))
    return jnp.split(m, 6, axis=-1)


def axial_rope_tables(rows):
    row = jnp.repeat(jnp.arange(rows, dtype=jnp.float32), GRID_W)
    col = jnp.tile(jnp.arange(GRID_W, dtype=jnp.float32), rows)
    n_freq = HEAD_DIM // 4
    inv = ROPE_THETA ** (-jnp.arange(n_freq, dtype=jnp.float32) / n_freq)
    ang = jnp.concatenate([row[:, None] * inv, col[:, None] * inv], axis=-1)
    return jnp.cos(ang), jnp.sin(ang)


def apply_rope(x, cos, sin):
    xf = x.astype(jnp.float32).reshape(x.shape[:-1] + (HEAD_DIM // 2, 2))
    x0, x1 = xf[..., 0], xf[..., 1]
    cs, sn = cos[None, :, None, :], sin[None, :, None, :]
    out = jnp.stack([x0 * cs - x1 * sn, x0 * sn + x1 * cs], axis=-1)
    return out.reshape(x.shape).astype(x.dtype)


def project_in(h, w_in, q_norm_g, k_norm_g):
    B, T = h.shape[:2]
    xr, gr, q, k, v, ga, gl = jnp.split(h @ w_in, SPLIT_AT, axis=-1)
    q = rms_norm(q.reshape(B, T, N_Q_HEADS, HEAD_DIM), q_norm_g)
    k = rms_norm(k.reshape(B, T, N_KV_HEADS, HEAD_DIM), k_norm_g)
    v = v.reshape(B, T, N_KV_HEADS, HEAD_DIM)
    return xr, gr, q, k, v, ga, gl


def _attend(q, k, v):
    s = jnp.einsum('bqkgd,bskd->bkgqs', q, k).astype(jnp.float32) * (HEAD_DIM ** -0.5)
    p = jax.nn.softmax(s, axis=-1).astype(v.dtype)
    return jnp.einsum('bkgqs,bskd->bqkgd', p, v)


def gqa_attention(q_l, k_l, v_l, q_c, k_c, v_c, need_ctx):
    B, N = q_l.shape[:2]
    k_all = jnp.concatenate([k_c, k_l], axis=1)
    v_all = jnp.concatenate([v_c, v_l], axis=1)
    nb = N // Q_BLOCK
    qb = q_l.reshape(B, nb, Q_BLOCK, N_KV_HEADS, GQA_GROUP, HEAD_DIM).transpose(1, 0, 2, 3, 4, 5)
    o_l = lax.map(lambda qi: _attend(qi, k_all, v_all), qb)
    o_l = o_l.transpose(1, 0, 2, 3, 4, 5).reshape(B, N, Q_WIDTH)
    o_c = None
    if need_ctx:
        T = q_c.shape[1]
        o_c = _attend(q_c.reshape(B, T, N_KV_HEADS, GQA_GROUP, HEAD_DIM), k_c, v_c).reshape(B, T, Q_WIDTH)
    return o_l, o_c


def centred_conv(x, w, b):
    n = x.shape[1]
    xp = jnp.pad(x, ((0, 0), (CONV_LEFT, CONV_W - 1 - CONV_LEFT), (0, 0)))
    return b + sum(xp[:, j:j + n] * w[j] for j in range(CONV_W))


def block_diag(x, w, b):
    xb = x.reshape(x.shape[:-1] + (LRU_BLOCKS, LRU_BLOCK))
    return jnp.einsum('bnhi,hij->bnhj', xb, w).reshape(x.shape) + b


def rglru_coeffs(u, w_r, b_r, w_i, b_i, lam):
    r = jax.nn.sigmoid(block_diag(u, w_r, b_r)).astype(jnp.float32)
    i = jax.nn.sigmoid(block_diag(u, w_i, b_i)).astype(jnp.float32)
    log_a = -LRU_C * r * jax.nn.softplus(-lam.astype(jnp.float32))
    a = jnp.exp(log_a)
    return a, jnp.sqrt(-jnp.expm1(2.0 * log_a)) * i * u.astype(jnp.float32)


def _combine(e1, e2):
    a1, b1 = e1
    a2, b2 = e2
    return a1 * a2, a2 * b1 + b2


def linear_scan(a, b, h0, reverse):
    edge = -1 if reverse else 0
    b = b.at[:, edge].add(a[:, edge] * h0)
    _, h = lax.associative_scan(_combine, (a, b), reverse=reverse, axis=1)
    return h


def rglru_branch(xr_l, gr_l, xr_c, gr_c, conv_w, conv_b, w_rg, b_rg, w_ig, b_ig, lru_lambda, need_ctx):
    u_l = centred_conv(xr_l, conv_w, conv_b)
    u_c = centred_conv(xr_c, conv_w, conv_b)
    ys_l, ys_c = [], []
    for d, rev in enumerate((False, True)):
        prm = (w_rg[d], b_rg[d], w_ig[d], b_ig[d], lru_lambda[d])
        a_c, b_c = rglru_coeffs(u_c, *prm)
        h_c = linear_scan(a_c, b_c, jnp.zeros_like(a_c[:, 0]), rev)
        a_l, b_l = rglru_coeffs(u_l, *prm)
        h_l = linear_scan(a_l, b_l, h_c[:, 0 if rev else -1], rev)
        ys_l.append(h_l)
        ys_c.append(h_c)
    y_l = ((ys_l[0] + ys_l[1]) * jax.nn.gelu(gr_l.astype(jnp.float32))).astype(xr_l.dtype)
    y_c = None
    if need_ctx:
        y_c = ((ys_c[0] + ys_c[1]) * jax.nn.gelu(gr_c.astype(jnp.float32))).astype(xr_c.dtype)
    return y_l, y_c


def merge_branches(y_attn, y_lru, ga, gl, w_branch, w_out):
    m = jax.nn.sigmoid(ga) * (y_attn @ w_branch[0]) + jax.nn.sigmoid(gl) * (y_lru @ w_branch[1])
    return m @ w_out


def hybrid_mixer(hl, hc, cos, sin, w_in, q_norm_g, k_norm_g, conv_w, conv_b, w_rg, b_rg, w_ig, b_ig, lru_lambda, w_branch, w_out, need_ctx):
    xr_l, gr_l, q_l, k_l, v_l, ga_l, gl_l = project_in(hl, w_in, q_norm_g, k_norm_g)
    xr_c, gr_c, q_c, k_c, v_c, ga_c, gl_c = project_in(hc, w_in, q_norm_g, k_norm_g)
    q_l = apply_rope(q_l, cos, sin)
    k_l = apply_rope(k_l, cos, sin)
    att_l, att_c = gqa_attention(q_l, k_l, v_l, q_c, k_c, v_c, need_ctx)
    lru_l, lru_c = rglru_branch(xr_l, gr_l, xr_c, gr_c, conv_w, conv_b, w_rg, b_rg, w_ig, b_ig, lru_lambda, need_ctx)
    out_l = merge_branches(att_l, lru_l, ga_l, gl_l, w_branch, w_out)
    out_c = merge_branches(att_c, lru_c, ga_c, gl_c, w_branch, w_out) if need_ctx else None
    return out_l, out_c


def expert_choice_ffn(h, w_router, w_gate_up, w_down):
    T = h.shape[1]
    cap = CAP_FACTOR * T // N_EXPERTS
    aff = jax.nn.softmax((h @ w_router).astype(jnp.float32), axis=-1)
    gate, idx = lax.top_k(jnp.swapaxes(aff, 1, 2), cap)
    xs = jax.vmap(lambda hb, ib: hb[ib])(h, idx)
    g_, u_ = jnp.split(jnp.einsum('becd,edf->becf', xs, w_gate_up), 2, axis=-1)
    out = jnp.einsum('becf,efd->becd', jax.nn.silu(g_) * u_, w_down) * gate[..., None].astype(h.dtype)
    return jax.vmap(lambda ib, ob: jnp.zeros((T, D_MODEL), h.dtype).at[ib.reshape(-1)].add(ob.reshape(-1, D_MODEL)))(idx, out)


def setup_inputs(seed: int = 0) -> dict:
    key = jax.random.key(seed)
    ks = jax.random.split(key, 24)
    nrm = jax.random.normal
    lam_u = jax.random.uniform(ks[12], (DEPTH, 2, LRU_WIDTH), minval=0.9, maxval=0.999)
    lam_s = lam_u ** (1.0 / LRU_C)
    return {
        "x": nrm(ks[0], (BATCH, SEQ, D_MODEL), jnp.float32),
        "c": nrm(ks[1], (BATCH, D_MODEL), jnp.float32),
        "ctx": nrm(ks[2], (BATCH, CTX_LEN, D_MODEL), jnp.float32),
        "c_ctx": nrm(ks[3], (D_MODEL,), jnp.float32),
        "w_ada_dn": nrm(ks[4], (DEPTH, D_MODEL, ADA_RANK), jnp.float32) * D_MODEL ** -0.5,
        "w_ada_up": nrm(ks[5], (DEPTH, ADA_RANK, 6 * D_MODEL), jnp.float32) * ADA_RANK ** -0.5,
        "b_ada": nrm(ks[6], (DEPTH, 6 * D_MODEL), jnp.float32) * 0.01,
        "w_in": nrm(ks[7], (DEPTH, D_MODEL, IN_WIDTH), jnp.float32) * D_MODEL ** -0.5,
        "q_norm_g": 1.0 + 0.02 * nrm(ks[8], (DEPTH, HEAD_DIM), jnp.float32),
        "k_norm_g": 1.0 + 0.02 * nrm(ks[9], (DEPTH, HEAD_DIM), jnp.float32),
        "conv_w": nrm(ks[10], (DEPTH, CONV_W, LRU_WIDTH), jnp.float32) * CONV_W ** -0.5,
        "conv_b": nrm(ks[11], (DEPTH, LRU_WIDTH), jnp.float32) * 0.01,
        "w_rg": nrm(ks[13], (DEPTH, 2, LRU_BLOCKS, LRU_BLOCK, LRU_BLOCK), jnp.float32) * LRU_BLOCK ** -0.5,
        "b_rg": nrm(ks[14], (DEPTH, 2, LRU_WIDTH), jnp.float32) * 0.01,
        "w_ig": nrm(ks[15], (DEPTH, 2, LRU_BLOCKS, LRU_BLOCK, LRU_BLOCK), jnp.float32) * LRU_BLOCK ** -0.5,
        "b_ig": nrm(ks[16], (DEPTH, 2, LRU_WIDTH), jnp.float32) * 0.01,
        "lru_lambda": jnp.log(lam_s) - jnp.log1p(-lam_s),
        "w_branch": nrm(ks[17], (DEPTH, 2, LRU_WIDTH, D_MODEL), jnp.float32) * (LRU_WIDTH ** -0.5 * BETA),
        "w_out": nrm(ks[18], (DEPTH, D_MODEL, D_MODEL), jnp.float32) * (D_MODEL ** -0.5 * BETA),
        "ln_g": 1.0 + 0.02 * nrm(ks[19], (DEPTH, 2, D_MODEL), jnp.float32),
        "ln_b": 0.01 * nrm(ks[20], (DEPTH, 2, D_MODEL), jnp.float32),
        "w_router": nrm(ks[21], (DEPTH, D_MODEL, N_EXPERTS), jnp.float32) * D_MODEL ** -0.5,
        "w_gate_up": nrm(ks[22], (DEPTH, N_EXPERTS, D_MODEL, 2 * D_EXPERT), jnp.float32) * D_MODEL ** -0.5,
        "w_down": nrm(ks[23], (DEPTH, N_EXPERTS, D_EXPERT, D_MODEL), jnp.float32) * (D_EXPERT ** -0.5 * BETA),
    }


def reference(x, c, ctx, c_ctx, w_ada_dn, w_ada_up, b_ada, w_in, q_norm_g, k_norm_g, conv_w, conv_b, w_rg, b_rg, w_ig, b_ig, lru_lambda, w_branch, w_out, ln_g, ln_b, w_router, w_gate_up, w_down):
    rows = x.shape[1] // GRID_W
    cos, sin = axial_rope_tables(rows)
    xl, xc = x, ctx
    for l in range(DEPTH):
        need_ctx = l < DEPTH - 1
        sh1, sc1, g1, sh2, sc2, g2 = ada_modulation(c, w_ada_dn[l], w_ada_up[l], b_ada[l])
        csh1, csc1, cg1, csh2, csc2, cg2 = ada_modulation(c_ctx, w_ada_dn[l], w_ada_up[l], b_ada[l])
        mix_l, mix_c = hybrid_mixer(modulate(xl, sh1, sc1), modulate(xc, csh1, csc1), cos, sin, w_in[l], q_norm_g[l], k_norm_g[l], conv_w[l], conv_b[l], w_rg[l], b_rg[l], w_ig[l], b_ig[l], lru_lambda[l], w_branch[l], w_out[l], need_ctx)
        xl = layer_norm(ALPHA * xl + g1 * mix_l, ln_g[l, 0], ln_b[l, 0])
        xl = layer_norm(ALPHA * xl + g2 * expert_choice_ffn(modulate(xl, sh2, sc2), w_router[l], w_gate_up[l], w_down[l]), ln_g[l, 1], ln_b[l, 1])
        if need_ctx:
            xc = layer_norm(ALPHA * xc + cg1 * mix_c, ln_g[l, 0], ln_b[l, 0])
            xc = layer_norm(ALPHA * xc + cg2 * expert_choice_ffn(modulate(xc, csh2, csc2), w_router[l], w_gate_up[l], w_down[l]), ln_g[l, 1], ln_b[l, 1])
    return xl
```

```python
import functools

import jax
import jax.numpy as jnp
from jax import lax
from jax.experimental import pallas as pl
from jax.experimental.pallas import tpu as pltpu

HEAD_DIM = 128
GQA_GROUP = 4
GRID_W = 64
ROPE_THETA = 10000.0
LRU_BLOCK = 128
LRU_C = 8.0
CONV_W = 4
CONV_LEFT = 2
N_EXPERTS = 16
CAP_FACTOR = 2
NORM_EPS = 1e-6

V7X_LANES = 128
V7X_SUBLANES = 8
V7X_VMEM_LIMIT_BYTES = 56 * 1024 * 1024

TOKEN_WINDOW = 256
SLOT_TILE = 128

F32 = jnp.float32
BF16 = jnp.bfloat16
HIGHEST = lax.Precision.HIGHEST


def _tile(n, pref):
    t = min(pref, n)
    while n % t:
        t //= 2
    return t


def _params(*sem):
    return pltpu.CompilerParams(dimension_semantics=sem, vmem_limit_bytes=V7X_VMEM_LIMIT_BYTES)


def _sigmoid(x):
    return 1.0 / (1.0 + jnp.exp(-x))


def _dot(a, b):
    return jnp.dot(a, b, preferred_element_type=F32)


def _ada_kernel(c_ref, wdn_ref, wup_ref, b_ref, o_ref):
    c = c_ref[...]
    s = c * _sigmoid(c)
    t = jnp.dot(s, wdn_ref[0], preferred_element_type=F32, precision=HIGHEST)
    o_ref[0] = jnp.dot(t, wup_ref[0], preferred_element_type=F32, precision=HIGHEST) + b_ref[0]


def ada_modulation(cv, w_dn, w_up, b_up):
    L, D, R = w_dn.shape
    N = w_up.shape[2]
    rows = cv.shape[0]
    tn = _tile(N, 4096)
    return pl.pallas_call(
        _ada_kernel,
        grid=(L, N // tn),
        in_specs=[
            pl.BlockSpec((rows, D), lambda l, j: (0, 0)),
            pl.BlockSpec((1, D, R), lambda l, j: (l, 0, 0)),
            pl.BlockSpec((1, R, tn), lambda l, j: (l, 0, j)),
            pl.BlockSpec((1, 1, tn), lambda l, j: (l, 0, j)),
        ],
        out_specs=pl.BlockSpec((1, rows, tn), lambda l, j: (l, 0, j)),
        out_shape=jax.ShapeDtypeStruct((L, rows, N), F32),
        compiler_params=_params("arbitrary", "arbitrary"),
        name="ada_modulation",
    )(cv, w_dn, w_up, b_up.reshape(L, 1, N))


def _modulate_kernel(x_ref, mod_ref, h_ref):
    sh = mod_ref[0, 0:1, :]
    sc = mod_ref[0, 1:2, :]
    h_ref[0] = (x_ref[0] * (1.0 + sc) + sh).astype(BF16)


def modulate_rows(x, mod):
    B, T, D = x.shape
    tr = _tile(T, 512)
    per_batch = mod.shape[0] > 1
    return pl.pallas_call(
        _modulate_kernel,
        grid=(B, T // tr),
        in_specs=[
            pl.BlockSpec((1, tr, D), lambda b, i: (b, i, 0)),
            pl.BlockSpec((1, 6, D), (lambda b, i: (b, 0, 0)) if per_batch else (lambda b, i: (0, 0, 0))),
        ],
        out_specs=pl.BlockSpec((1, tr, D), lambda b, i: (b, i, 0)),
        out_shape=jax.ShapeDtypeStruct((B, T, D), BF16),
        compiler_params=_params("arbitrary", "arbitrary"),
        name="modulate",
    )(x, mod)


def _matmul_kernel(a_ref, w_ref, o_ref):
    o_ref[...] = _dot(a_ref[...], w_ref[...]).astype(o_ref.dtype)


def matmul(a, w, out_dtype=BF16):
    M, K = a.shape
    N = w.shape[1]
    tm, tn = _tile(M, 1024), _tile(N, 1024)
    return pl.pallas_call(
        _matmul_kernel,
        grid=(M // tm, N // tn),
        in_specs=[
            pl.BlockSpec((tm, K), lambda i, j: (i, 0)),
            pl.BlockSpec((K, tn), lambda i, j: (0, j)),
        ],
        out_specs=pl.BlockSpec((tm, tn), lambda i, j: (i, j)),
        out_shape=jax.ShapeDtypeStruct((M, N), out_dtype),
        compiler_params=_params("arbitrary", "arbitrary"),
        name="matmul",
    )(a, w)


def _normrope_kernel(*refs, nq, nkv, rope, scale):
    if rope:
        q_ref, kv_ref, qg_ref, kg_ref, cos_ref, sin_ref, qo_ref, ko_ref, vo_ref = refs
        cs = cos_ref[...]
        sn = sin_ref[...]
        lane = lax.broadcasted_iota(jnp.int32, cs.shape, 1)
        even = (lane & 1) == 0
    else:
        q_ref, kv_ref, qg_ref, kg_ref, qo_ref, ko_ref, vo_ref = refs

    def head(xh, g):
        xf = xh.astype(F32)
        y = xf * lax.rsqrt(jnp.mean(xf * xf, axis=-1, keepdims=True) + NORM_EPS) * g
        if rope:
            partner = jnp.where(even, pltpu.roll(y, HEAD_DIM - 1, 1), pltpu.roll(y, 1, 1))
            y = y * cs + partner * sn
        return y

    qg = qg_ref[...]
    kg = kg_ref[...]
    for h in range(nq):
        sl = slice(h * HEAD_DIM, (h + 1) * HEAD_DIM)
        qo_ref[0, :, sl] = (head(q_ref[0, :, sl], qg) * scale).astype(BF16)
    for h in range(nkv):
        sl = slice(h * HEAD_DIM, (h + 1) * HEAD_DIM)
        ko_ref[0, :, sl] = head(kv_ref[0, :, sl], kg).astype(BF16)
    vo_ref[0] = kv_ref[0, :, nkv * HEAD_DIM:]


def norm_rope(proj, q_off, qw, kvw, qg, kg, cos2, sin2):
    B, T, _ = proj.shape
    rope = cos2 is not None
    tr = _tile(T, 512)
    assert q_off % qw == 0 and (q_off + qw) % (2 * kvw) == 0
    qb, kvb = q_off // qw, (q_off + qw) // (2 * kvw)
    in_specs = [
        pl.BlockSpec((1, tr, qw), lambda b, i: (b, i, qb)),
        pl.BlockSpec((1, tr, 2 * kvw), lambda b, i: (b, i, kvb)),
        pl.BlockSpec((1, HEAD_DIM), lambda b, i: (0, 0)),
        pl.BlockSpec((1, HEAD_DIM), lambda b, i: (0, 0)),
    ]
    args = [proj, proj, qg.reshape(1, HEAD_DIM), kg.reshape(1, HEAD_DIM)]
    if rope:
        in_specs += [pl.BlockSpec((tr, HEAD_DIM), lambda b, i: (i, 0))] * 2
        args += [cos2, sin2]
    kern = functools.partial(_normrope_kernel, nq=qw // HEAD_DIM, nkv=kvw // HEAD_DIM, rope=rope,
                             scale=HEAD_DIM ** -0.5)
    return pl.pallas_call(
        kern,
        grid=(B, T // tr),
        in_specs=in_specs,
        out_specs=[
            pl.BlockSpec((1, tr, qw), lambda b, i: (b, i, 0)),
            pl.BlockSpec((1, tr, kvw), lambda b, i: (b, i, 0)),
            pl.BlockSpec((1, tr, kvw), lambda b, i: (b, i, 0)),
        ],
        out_shape=[
            jax.ShapeDtypeStruct((B, T, qw), BF16),
            jax.ShapeDtypeStruct((B, T, kvw), BF16),
            jax.ShapeDtypeStruct((B, T, kvw), BF16),
        ],
        compiler_params=_params("arbitrary", "arbitrary"),
        name="norm_rope",
    )(*args)


def _attn_kernel(q_ref, k_ref, v_ref, o_ref, m_ref, l_ref, acc_ref, *, tq, first, tk, n_chunks):
    q = jnp.concatenate([q_ref[0, :, g * HEAD_DIM:(g + 1) * HEAD_DIM] for g in range(GQA_GROUP)], axis=0)
    m_ref[...] = jnp.full(m_ref.shape, -jnp.inf, F32)
    l_ref[...] = jnp.zeros(l_ref.shape, F32)
    acc_ref[...] = jnp.zeros(acc_ref.shape, F32)

    def step(k, v):
        s = lax.dot_general(q, k, (((1,), (1,)), ((), ())), preferred_element_type=F32)
        m_old = m_ref[...]
        m_new = jnp.maximum(m_old, jnp.max(s, axis=-1, keepdims=True))
        a = jnp.exp(m_old - m_new)
        p = jnp.exp(s - m_new)
        l_ref[...] = a * l_ref[...] + jnp.sum(p, axis=-1, keepdims=True)
        acc_ref[...] = a * acc_ref[...] + _dot(p.astype(BF16), v)
        m_ref[...] = m_new

    step(k_ref[0, 0:first, :], v_ref[0, 0:first, :])

    def body(i, carry):
        off = pl.multiple_of(first + i * tk, tk if first % tk == 0 else first)
        step(k_ref[0, pl.ds(off, tk), :], v_ref[0, pl.ds(off, tk), :])
        return carry

    if n_chunks:
        lax.fori_loop(0, n_chunks, body, 0)
    o = acc_ref[...] * (1.0 / l_ref[...])
    for g in range(GQA_GROUP):
        o_ref[0, :, g * HEAD_DIM:(g + 1) * HEAD_DIM] = o[g * tq:(g + 1) * tq].astype(BF16)


def attention(q, k, v, first, tk):
    B, Tq, QW = q.shape
    Tk, KVW = k.shape[1], k.shape[2]
    nkv = KVW // HEAD_DIM
    tq = _tile(Tq, 256)
    n_chunks = (Tk - first) // tk if Tk > first else 0
    assert first + n_chunks * tk == Tk
    gw = GQA_GROUP * HEAD_DIM
    kern = functools.partial(_attn_kernel, tq=tq, first=first, tk=tk, n_chunks=n_chunks)
    return pl.pallas_call(
        kern,
        grid=(B, nkv, Tq // tq),
        in_specs=[
            pl.BlockSpec((1, tq, gw), lambda b, h, i: (b, i, h)),
            pl.BlockSpec((1, Tk, HEAD_DIM), lambda b, h, i: (b, 0, h)),
            pl.BlockSpec((1, Tk, HEAD_DIM), lambda b, h, i: (b, 0, h)),
        ],
        out_specs=pl.BlockSpec((1, tq, gw), lambda b, h, i: (b, i, h)),
        out_shape=jax.ShapeDtypeStruct((B, Tq, QW), BF16),
        scratch_shapes=[
            pltpu.VMEM((GQA_GROUP * tq, 1), F32),
            pltpu.VMEM((GQA_GROUP * tq, 1), F32),
            pltpu.VMEM((GQA_GROUP * tq, HEAD_DIM), F32),
        ],
        compiler_params=_params("arbitrary", "arbitrary", "arbitrary"),
        name="attention",
    )(q, k, v)


def _gelu_tanh(x):
    return 0.5 * x * (1.0 + jnp.tanh(0.7978845608028654 * (x + 0.044715 * x * x * x)))


def _lru_kernel(xl_ref, gl_ref, xc_ref, gc_ref, cw_ref, cb_ref, wr_ref, br_ref, wi_ref, bi_ref, lam_ref,
                yl_ref, yc_ref, pad_ref, al_ref, bl_ref, hl_ref, pl_ref, ac_ref, bc_ref, hc_ref, pc_ref,
                *, Ll, Lc):
    S = V7X_SUBLANES
    cw = cw_ref[...]
    cb = cb_ref[...]

    def coeffs(x_ref, L, a_ref, b_ref):
        pad_ref[0:S, :] = jnp.zeros((S, LRU_BLOCK), F32)
        pad_ref[S:S + L, :] = x_ref[0].astype(F32)
        pad_ref[S + L:2 * S + L, :] = jnp.zeros((S, LRU_BLOCK), F32)
        u = cb
        for j in range(CONV_W):
            start = S - CONV_LEFT + j
            u = u + pad_ref[start:start + L, :] * cw[j:j + 1, :]
        ub = u.astype(BF16)
        for d in range(2):
            r = _sigmoid(_dot(ub, wr_ref[d, 0]) + br_ref[d:d + 1, :])
            i = _sigmoid(_dot(ub, wi_ref[d, 0]) + bi_ref[d:d + 1, :])
            nl = -lam_ref[d:d + 1, :]
            softplus = jnp.maximum(nl, 0.0) + jnp.log(1.0 + jnp.exp(-jnp.abs(nl)))
            log_a = (-LRU_C) * r * softplus
            a = jnp.exp(log_a)
            a_ref[d] = a
            b_ref[d] = jnp.sqrt(1.0 - a * a) * i * u

    def local_scan(L, a_ref, b_ref, h_ref, p_ref):
        n = L // S

        def body(s, carry):
            hf, pf, hr, pr = carry
            af = a_ref[0, pl.ds(s, S, stride=n), :]
            bf = b_ref[0, pl.ds(s, S, stride=n), :]
            hf = af * hf + bf
            pf = af * pf
            h_ref[0, pl.ds(s, S, stride=n), :] = hf
            p_ref[0, pl.ds(s, S, stride=n), :] = pf
            t = n - 1 - s
            ar = a_ref[1, pl.ds(t, S, stride=n), :]
            brv = b_ref[1, pl.ds(t, S, stride=n), :]
            hr = ar * hr + brv
            pr = ar * pr
            h_ref[1, pl.ds(t, S, stride=n), :] = hr
            p_ref[1, pl.ds(t, S, stride=n), :] = pr
            return hf, pf, hr, pr

        z = jnp.zeros((S, LRU_BLOCK), F32)
        o = jnp.ones((S, LRU_BLOCK), F32)
        return lax.fori_loop(0, n, body, (z, o, z, o))

    def resolve(h_end, p_end, init, reverse):
        order = range(S - 1, -1, -1) if reverse else range(S)
        inits = [None] * S
        cur = init
        for c in order:
            inits[c] = cur
            cur = h_end[c:c + 1, :] + p_end[c:c + 1, :] * cur
        return inits, cur

    def finish(L, h_ref, p_ref, inits_f, inits_r, g_ref, y_ref):
        n = L // S
        for c in range(S):
            rows = slice(c * n, (c + 1) * n)
            hf = h_ref[0, rows, :] + p_ref[0, rows, :] * inits_f[c]
            hr = h_ref[1, rows, :] + p_ref[1, rows, :] * inits_r[c]
            y_ref[0, rows, :] = ((hf + hr) * _gelu_tanh(g_ref[0, rows, :].astype(F32))).astype(BF16)

    coeffs(xc_ref, Lc, ac_ref, bc_ref)
    coeffs(xl_ref, Ll, al_ref, bl_ref)
    hcf, pcf, hcr, pcr = local_scan(Lc, ac_ref, bc_ref, hc_ref, pc_ref)
    hlf, plf, hlr, plr = local_scan(Ll, al_ref, bl_ref, hl_ref, pl_ref)
    zero = jnp.zeros((1, LRU_BLOCK), F32)
    ic_f, end_cf = resolve(hcf, pcf, zero, False)
    ic_r, end_cr = resolve(hcr, pcr, zero, True)
    il_f, _ = resolve(hlf, plf, end_cf, False)
    il_r, _ = resolve(hlr, plr, end_cr, True)
    finish(Lc, hc_ref, pc_ref, ic_f, ic_r, gc_ref, yc_ref)
    finish(Ll, hl_ref, pl_ref, il_f, il_r, gl_ref, yl_ref)


def lru_branch(proj_l, proj_c, lru_w, conv_w, conv_b, w_rg, b_rg, w_ig, b_ig, lam):
    B, Ll, _ = proj_l.shape
    Lc = proj_c.shape[1]
    nb = lru_w // LRU_BLOCK
    S = V7X_SUBLANES
    blk = lambda L, off: pl.BlockSpec((1, L, LRU_BLOCK), lambda b, h: (b, 0, h + off))
    chan = lambda rows: pl.BlockSpec((rows, LRU_BLOCK), lambda b, h: (0, h))
    wspec = pl.BlockSpec((2, 1, LRU_BLOCK, LRU_BLOCK), lambda b, h: (0, h, 0, 0))
    kern = functools.partial(_lru_kernel, Ll=Ll, Lc=Lc)
    return pl.pallas_call(
        kern,
        grid=(B, nb),
        in_specs=[blk(Ll, 0), blk(Ll, nb), blk(Lc, 0), blk(Lc, nb), chan(CONV_W), chan(1),
                  wspec, chan(2), wspec, chan(2), chan(2)],
        out_specs=[pl.BlockSpec((1, Ll, LRU_BLOCK), lambda b, h: (b, 0, h)),
                   pl.BlockSpec((1, Lc, LRU_BLOCK), lambda b, h: (b, 0, h))],
        out_shape=[jax.ShapeDtypeStruct((B, Ll, lru_w), BF16), jax.ShapeDtypeStruct((B, Lc, lru_w), BF16)],
        scratch_shapes=[pltpu.VMEM((Ll + 2 * S, LRU_BLOCK), F32)]
        + [pltpu.VMEM((2, Ll, LRU_BLOCK), F32)] * 4 + [pltpu.VMEM((2, Lc, LRU_BLOCK), F32)] * 4,
        compiler_params=_params("arbitrary", "arbitrary"),
        name="rglru",
    )(proj_l, proj_l, proj_c, proj_c, conv_w, conv_b.reshape(1, lru_w), w_rg, b_rg, w_ig, b_ig, lam)


def _merge_kernel(att_ref, lru_ref, ga_ref, gl_ref, wa_ref, wl_ref, o_ref):
    ya = _dot(att_ref[...], wa_ref[0])
    yl = _dot(lru_ref[...], wl_ref[0])
    m = _sigmoid(ga_ref[...].astype(F32)) * ya + _sigmoid(gl_ref[...].astype(F32)) * yl
    o_ref[...] = m.astype(BF16)


def merge_branches(att, lru, proj, ga_off, D, w_branch):
    M, K = att.shape
    tm = _tile(M, 1024)
    tn = 512
    while ga_off % tn or D % tn:
        tn //= 2
    ga_b, gl_b = ga_off // tn, (ga_off + D) // tn
    return pl.pallas_call(
        _merge_kernel,
        grid=(M // tm, D // tn),
        in_specs=[
            pl.BlockSpec((tm, K), lambda i, j: (i, 0)),
            pl.BlockSpec((tm, K), lambda i, j: (i, 0)),
            pl.BlockSpec((tm, tn), lambda i, j: (i, ga_b + j)),
            pl.BlockSpec((tm, tn), lambda i, j: (i, gl_b + j)),
            pl.BlockSpec((1, K, tn), lambda i, j: (0, 0, j)),
            pl.BlockSpec((1, K, tn), lambda i, j: (1, 0, j)),
        ],
        out_specs=pl.BlockSpec((tm, tn), lambda i, j: (i, j)),
        out_shape=jax.ShapeDtypeStruct((M, D), BF16),
        compiler_params=_params("arbitrary", "arbitrary"),
        name="merge",
    )(att, lru, proj, proj, w_branch, w_branch)


def _postln_kernel(*refs, gate_row, sh_row, sc_row, alpha, router, emit_h):
    x_ref, y_ref, mod_ref, modn_ref, lng_ref, lnb_ref = refs[:6]
    rest = list(refs[6:])
    wr_ref = rest.pop(0) if router else None
    xo_ref = rest.pop(0)
    ho_ref = rest.pop(0) if emit_h else None
    lg_ref = rest.pop(0) if router else None
    z = alpha * x_ref[0] + mod_ref[0, gate_row:gate_row + 1, :] * y_ref[0].astype(F32)
    mu = jnp.mean(z, axis=-1, keepdims=True)
    zc = z - mu
    var = jnp.mean(zc * zc, axis=-1, keepdims=True)
    xn = zc * lax.rsqrt(var + NORM_EPS) * lng_ref[...] + lnb_ref[...]
    xo_ref[0] = xn
    if emit_h:
        h = xn * (1.0 + modn_ref[0, sc_row:sc_row + 1, :]) + modn_ref[0, sh_row:sh_row + 1, :]
        ho_ref[0] = h.astype(BF16)
        if router:
            lg_ref[0] = lax.dot_general(wr_ref[...], h, (((1,), (1,)), ((), ())),
                                        preferred_element_type=F32, precision=HIGHEST)


def post_ln(x, y, mod, modn, ln_g, ln_b, alpha, gate_row, sh_row, sc_row, w_router_t=None, emit_h=True):
    B, T, D = x.shape
    tr = _tile(T, 256)
    router = w_router_t is not None
    mspec = lambda m: pl.BlockSpec((1, 6, D), (lambda b, i: (b, 0, 0)) if m.shape[0] > 1 else (lambda b, i: (0, 0, 0)))
    row = pl.BlockSpec((1, tr, D), lambda b, i: (b, i, 0))
    vec = pl.BlockSpec((1, D), lambda b, i: (0, 0))
    in_specs = [row, row, mspec(mod), mspec(modn), vec, vec]
    args = [x, y, mod, modn, ln_g.reshape(1, D), ln_b.reshape(1, D)]
    out_specs = [row]
    out_shape = [jax.ShapeDtypeStruct((B, T, D), F32)]
    if router:
        E = w_router_t.shape[0]
        in_specs.append(pl.BlockSpec((E, D), lambda b, i: (0, 0)))
        args.append(w_router_t)
    if emit_h:
        out_specs.append(row)
        out_shape.append(jax.ShapeDtypeStruct((B, T, D), BF16))
    if router:
        out_specs.append(pl.BlockSpec((1, E, tr), lambda b, i: (b, 0, i)))
        out_shape.append(jax.ShapeDtypeStruct((B, E, T), F32))
    kern = functools.partial(_postln_kernel, gate_row=gate_row, sh_row=sh_row, sc_row=sc_row, alpha=alpha,
                             router=router, emit_h=emit_h)
    return pl.pallas_call(
        kern,
        grid=(B, T // tr),
        in_specs=in_specs,
        out_specs=out_specs,
        out_shape=out_shape,
        compiler_params=_params("arbitrary", "arbitrary"),
        name="post_ln",
    )(*args)


def _topk_kernel(lg_ref, pos_ref, gate_ref, cnt_ref, *, T, cap, tw):
    lg = lg_ref[0]
    E = lg.shape[0]
    ex = jnp.exp(lg - jnp.max(lg, axis=0, keepdims=True))
    aff = ex / jnp.sum(ex, axis=0, keepdims=True)
    bits = lax.bitcast_convert_type(aff, jnp.int32)

    def count(mask):
        return jnp.sum(jnp.where(mask, 1.0, 0.0), axis=1, keepdims=True)

    thr = jnp.zeros((E, 1), jnp.int32)
    for bit in range(30, -1, -1):
        cand = thr | (1 << bit)
        thr = jnp.where(count(bits >= cand) >= cap, cand, thr)

    t_row = lax.broadcasted_iota(jnp.int32, (T, V7X_LANES), 0)
    w_col = lax.broadcasted_iota(jnp.int32, (T, V7X_LANES), 1)
    before_window = jnp.where(t_row < w_col * tw, 1.0, 0.0).astype(BF16)
    r_i = lax.broadcasted_iota(jnp.int32, (tw, tw), 0)
    c_i = lax.broadcasted_iota(jnp.int32, (tw, tw), 1)
    strict_upper = jnp.where(r_i < c_i, 1.0, 0.0).astype(BF16)

    def prefix(mask):
        mb = jnp.where(mask, 1.0, 0.0).astype(BF16)
        starts = _dot(mb, before_window)
        parts = [_dot(mb[:, w * tw:(w + 1) * tw], strict_upper) + starts[:, w:w + 1] for w in range(T // tw)]
        return jnp.concatenate(parts, axis=1) if len(parts) > 1 else parts[0], starts

    gt = bits > thr
    eq = bits == thr
    need = cap - count(gt)
    tie_rank, _ = prefix(eq)
    sel = gt | (eq & (tie_rank < need))
    slot, starts = prefix(sel)
    pos_ref[0] = jnp.where(sel, slot.astype(jnp.int32), -1)
    gate_ref[0] = jnp.where(sel, aff, 0.0)
    cnt_ref[0] = starts.astype(jnp.int32)


def expert_topk(logits, cap):
    B, E, T = logits.shape
    tw = min(TOKEN_WINDOW, T)
    kern = functools.partial(_topk_kernel, T=T, cap=cap, tw=tw)
    spec = pl.BlockSpec((1, E, T), lambda b: (b, 0, 0))
    return pl.pallas_call(
        kern,
        grid=(B,),
        in_specs=[spec],
        out_specs=[spec, spec, pl.BlockSpec((1, E, V7X_LANES), lambda b: (b, 0, 0))],
        out_shape=[jax.ShapeDtypeStruct((B, E, T), jnp.int32), jax.ShapeDtypeStruct((B, E, T), F32),
                   jax.ShapeDtypeStruct((B, E, V7X_LANES), jnp.int32)],
        compiler_params=_params("arbitrary"),
        name="expert_topk",
    )(logits)


def _gather_kernel(cnt_ref, pos_ref, h_ref, xs_ref, acc_ref, *, C, st, tw, nw, E):
    b = pl.program_id(0)
    e = pl.program_id(2)
    base = (b * E + e) * (nw + 1)
    acc_ref[...] = jnp.zeros(acc_ref.shape, F32)
    slot_i = lax.broadcasted_iota(jnp.int32, (st, tw), 0)
    for s in range(C // st):
        j0 = s * st

        def wbody(w, carry):
            lo = cnt_ref[base + w]
            hi = cnt_ref[base + w + 1]

            @pl.when((lo < j0 + st) & (hi > j0))
            def _():
                p = pos_ref[0, 0, pl.ds(w, 1), :]
                onehot = jnp.where(p == slot_i + j0, 1.0, 0.0).astype(BF16)
                off = pl.multiple_of(w * tw, tw)
                acc_ref[j0:j0 + st, :] += _dot(onehot, h_ref[0, pl.ds(off, tw), :])
            return carry

        lax.fori_loop(0, nw, wbody, 0)
    xs_ref[0, 0] = acc_ref[...].astype(BF16)


def gather_tokens(h, pos, cnt_flat, C):
    B, T, D = h.shape
    E = pos.shape[1]
    tw = min(TOKEN_WINDOW, T)
    nw = T // tw
    st = min(SLOT_TILE, C)
    dc = _tile(D, 1024)
    kern = functools.partial(_gather_kernel, C=C, st=st, tw=tw, nw=nw, E=E)
    grid_spec = pltpu.PrefetchScalarGridSpec(
        num_scalar_prefetch=1,
        grid=(B, D // dc, E),
        in_specs=[
            pl.BlockSpec((1, 1, nw, tw), lambda b, d, e, cnt: (b, e, 0, 0)),
            pl.BlockSpec((1, T, dc), lambda b, d, e, cnt: (b, 0, d)),
        ],
        out_specs=pl.BlockSpec((1, 1, C, dc), lambda b, d, e, cnt: (b, e, 0, d)),
        scratch_shapes=[pltpu.VMEM((C, dc), F32)],
    )
    return pl.pallas_call(
        kern,
        grid_spec=grid_spec,
        out_shape=jax.ShapeDtypeStruct((B, E, C, D), BF16),
        compiler_params=_params("arbitrary", "arbitrary", "arbitrary"),
        name="moe_gather",
    )(cnt_flat, pos.reshape(B, E, nw, tw), h)


def _ffn_kernel(xs_ref, wgu_ref, wd_ref, y_ref, *, F):
    gu = _dot(xs_ref[0, 0], wgu_ref[0])
    g = gu[:, :F]
    u = gu[:, F:]
    act = (g * _sigmoid(g) * u).astype(BF16)
    y_ref[0, 0] = _dot(act, wd_ref[0]).astype(BF16)


def expert_ffn(xs, w_gate_up, w_down):
    B, E, C, D = xs.shape
    F = w_down.shape[1]
    kern = functools.partial(_ffn_kernel, F=F)
    return pl.pallas_call(
        kern,
        grid=(E, B),
        in_specs=[
            pl.BlockSpec((1, 1, C, D), lambda e, b: (b, e, 0, 0)),
            pl.BlockSpec((1, D, 2 * F), lambda e, b: (e, 0, 0)),
            pl.BlockSpec((1, F, D), lambda e, b: (e, 0, 0)),
        ],
        out_specs=pl.BlockSpec((1, 1, C, D), lambda e, b: (b, e, 0, 0)),
        out_shape=jax.ShapeDtypeStruct((B, E, C, D), BF16),
        compiler_params=_params("arbitrary", "arbitrary"),
        name="expert_ffn",
    )(xs, w_gate_up, w_down)


def _scatter_kernel(cnt_ref, post_ref, gatet_ref, y_ref, o_ref, acc_ref, *, st, tw, nw, E):
    b = pl.program_id(0)
    w = pl.program_id(2)
    acc_ref[...] = jnp.zeros(acc_ref.shape, F32)
    slot_i = lax.broadcasted_iota(jnp.int32, (tw, st), 1)
    for e in range(E):
        base = (b * E + e) * (nw + 1)
        lo = cnt_ref[base + w]
        hi = cnt_ref[base + w + 1]
        s_lo = lo // st
        s_hi = jnp.where(hi > lo, (hi + st - 1) // st, s_lo)
        pcol = post_ref[0, :, e:e + 1]
        gcol = gatet_ref[0, :, e:e + 1]

        def sbody(s, carry):
            j0 = pl.multiple_of(s * st, st)
            onehot = jnp.where(pcol == slot_i + j0, 1.0, 0.0).astype(BF16)
            acc_ref[...] += gcol * _dot(onehot, y_ref[0, e, pl.ds(j0, st), :])
            return carry

        lax.fori_loop(s_lo, s_hi, sbody, 0)
    o_ref[0] = acc_ref[...].astype(o_ref.dtype)


def scatter_tokens(y, pos_t, gate_t, cnt_flat):
    B, E, C, D = y.shape
    T = pos_t.shape[1]
    tw = min(TOKEN_WINDOW, T)
    nw = T // tw
    st = min(SLOT_TILE, C)
    dc = _tile(D, 512)
    kern = functools.partial(_scatter_kernel, st=st, tw=tw, nw=nw, E=E)
    grid_spec = pltpu.PrefetchScalarGridSpec(
        num_scalar_prefetch=1,
        grid=(B, D // dc, nw),
        in_specs=[
            pl.BlockSpec((1, tw, E), lambda b, d, w, cnt: (b, w, 0)),
            pl.BlockSpec((1, tw, E), lambda b, d, w, cnt: (b, w, 0)),
            pl.BlockSpec((1, E, C, dc), lambda b, d, w, cnt: (b, 0, 0, d)),
        ],
        out_specs=pl.BlockSpec((1, tw, dc), lambda b, d, w, cnt: (b, w, d)),
        scratch_shapes=[pltpu.VMEM((tw, dc), F32)],
    )
    return pl.pallas_call(
        kern,
        grid_spec=grid_spec,
        out_shape=jax.ShapeDtypeStruct((B, T, D), BF16),
        compiler_params=_params("arbitrary", "arbitrary", "arbitrary"),
        name="moe_scatter",
    )(cnt_flat, pos_t, gate_t, y)


def expert_choice_ffn(h, logits, w_gate_up, w_down):
    B, T, D = h.shape
    E = logits.shape[1]
    cap = CAP_FACTOR * T // E
    nw = T // min(TOKEN_WINDOW, T)
    pos, gate, cnt = expert_topk(logits, cap)
    cnt_flat = cnt[:, :, :nw + 1].reshape(-1)
    xs = gather_tokens(h, pos, cnt_flat, cap)
    y = expert_ffn(xs, w_gate_up, w_down)
    return scatter_tokens(y, jnp.swapaxes(pos, 1, 2), jnp.swapaxes(gate, 1, 2), cnt_flat)


def _rope_tables(T):
    rows = T // GRID_W
    row = jnp.repeat(jnp.arange(rows, dtype=F32), GRID_W)
    col = jnp.tile(jnp.arange(GRID_W, dtype=F32), rows)
    n_freq = HEAD_DIM // 4
    inv = ROPE_THETA ** (-jnp.arange(n_freq, dtype=F32) / n_freq)
    ang = jnp.concatenate([row[:, None] * inv, col[:, None] * inv], axis=-1)
    cos2 = jnp.repeat(jnp.cos(ang), 2, axis=-1)
    sin = jnp.sin(ang)
    sin2 = jnp.stack([-sin, sin], axis=-1).reshape(T, HEAD_DIM)
    return cos2, sin2


def kernel(x, c, ctx, c_ctx, w_ada_dn, w_ada_up, b_ada, w_in, q_norm_g, k_norm_g, conv_w, conv_b, w_rg, b_rg, w_ig, b_ig, lru_lambda, w_branch, w_out, ln_g, ln_b, w_router, w_gate_up, w_down):
    B, T, D = x.shape
    Tc = ctx.shape[1]
    depth = w_in.shape[0]
    in_w = w_in.shape[2]
    lru_w = conv_w.shape[2]
    kvw = (in_w - 2 * lru_w - 2 * D) // (GQA_GROUP + 2)
    qw = GQA_GROUP * kvw
    q_off = 2 * lru_w
    ga_off = q_off + qw + 2 * kvw
    alpha = (2.0 * depth) ** 0.25

    n_rows = -(-(B + 1) // V7X_SUBLANES) * V7X_SUBLANES
    cv = jnp.zeros((n_rows, D), F32).at[:B].set(c).at[B].set(c_ctx)
    mods = ada_modulation(cv, w_ada_dn, w_ada_up, b_ada).reshape(depth, n_rows, 6, D)
    mod_l = [mods[l, :B] for l in range(depth)]
    mod_c = [mods[l, B:B + 1] for l in range(depth)]

    cos2, sin2 = _rope_tables(T)
    w_in_b = w_in.astype(BF16)
    w_branch_b = w_branch.astype(BF16)
    w_out_b = w_out.astype(BF16)
    w_rg_b = w_rg.astype(BF16)
    w_ig_b = w_ig.astype(BF16)
    w_gu_b = w_gate_up.astype(BF16)
    w_dn_b = w_down.astype(BF16)
    w_router_t = jnp.swapaxes(w_router, 1, 2)

    xl, xc = x, ctx
    hl = modulate_rows(xl, mod_l[0])
    hc = modulate_rows(xc, mod_c[0])
    for l in range(depth):
        need_ctx = l < depth - 1
        last = l == depth - 1
        proj_l = matmul(hl.reshape(B * T, D), w_in_b[l]).reshape(B, T, in_w)
        proj_c = matmul(hc.reshape(B * Tc, D), w_in_b[l]).reshape(B, Tc, in_w)
        q_l, k_l, v_l = norm_rope(proj_l, q_off, qw, kvw, q_norm_g[l], k_norm_g[l], cos2, sin2)
        q_c, k_c, v_c = norm_rope(proj_c, q_off, qw, kvw, q_norm_g[l], k_norm_g[l], None, None)
        k_all = jnp.concatenate([k_c, k_l], axis=1)
        v_all = jnp.concatenate([v_c, v_l], axis=1)
        att_l = attention(q_l, k_all, v_all, Tc, _tile(T, 512))
        lru_l, lru_c = lru_branch(proj_l, proj_c, lru_w, conv_w[l], conv_b[l], w_rg_b[l], b_rg[l], w_ig_b[l],
                                  b_ig[l], lru_lambda[l])

        def tail(xs, hs_proj, att, lru, mod, mod_next, Ts):
            m = merge_branches(att.reshape(B * Ts, qw), lru.reshape(B * Ts, lru_w), hs_proj.reshape(B * Ts, in_w),
                               ga_off, D, w_branch_b[l])
            mix = matmul(m, w_out_b[l]).reshape(B, Ts, D)
            x1, h2, logits = post_ln(xs, mix, mod, mod, ln_g[l, 0], ln_b[l, 0], alpha, 2, 3, 4, w_router_t[l])
            moe = expert_choice_ffn(h2, logits, w_gu_b[l], w_dn_b[l])
            if last:
                (x2,) = post_ln(x1, moe, mod, mod, ln_g[l, 1], ln_b[l, 1], alpha, 5, 0, 1, emit_h=False)
                return x2, None
            return post_ln(x1, moe, mod, mod_next, ln_g[l, 1], ln_b[l, 1], alpha, 5, 0, 1)

        xl, hl = tail(xl, proj_l, att_l, lru_l, mod_l[l], None if last else mod_l[l + 1], T)
        if need_ctx:
            att_c = attention(q_c, k_c, v_c, Tc, Tc)
            xc, hc = tail(xc, proj_c, att_c, lru_c, mod_c[l], mod_c[l + 1], Tc)
    return xl
```

```python
import functools
import math

import jax
import jax.numpy as jnp
from jax import lax
from jax.experimental import pallas as pl
from jax.experimental.pallas import tpu as pltpu

HEAD_DIM = 128
GQA_GROUP = 4
GRID_W = 64
ROPE_THETA = 10000.0
LRU_BLOCK = 128
LRU_C = 8.0
CONV_W = 4
CONV_LEFT = 2
N_EXPERTS = 16
CAP_FACTOR = 2
NORM_EPS = 1e-6

V7X_LANES = 128
V7X_SUBLANES = 8
V7X_VMEM_LIMIT_BYTES = 56 * 1024 * 1024

ATTN_TQ = 256
ATTN_TK = 1024
ATTN_SUB = 128
TOKEN_WINDOW = 256
SLOT_SPAN = 64
BF16_ROWS = 16

F32 = jnp.float32
BF16 = jnp.bfloat16
HIGHEST = lax.Precision.HIGHEST


def _tile(n, pref):
    t = min(pref, n)
    while n % t:
        t //= 2
    return t


def _params(*sem):
    return pltpu.CompilerParams(dimension_semantics=sem, vmem_limit_bytes=V7X_VMEM_LIMIT_BYTES)


def _sigmoid(x):
    return 1.0 / (1.0 + jnp.exp(-x))


def _dot(a, b):
    return jnp.dot(a, b, preferred_element_type=F32)


def _ada_kernel(c_ref, wdn_ref, wup_ref, b_ref, o_ref):
    c = c_ref[...]
    s = c * _sigmoid(c)
    t = jnp.dot(s, wdn_ref[0], preferred_element_type=F32, precision=HIGHEST)
    o_ref[0] = jnp.dot(t, wup_ref[0], preferred_element_type=F32, precision=HIGHEST) + b_ref[0]


def ada_modulation(cv, w_dn, w_up, b_up):
    L, D, R = w_dn.shape
    N = w_up.shape[2]
    rows = cv.shape[0]
    tn = _tile(N, 4096)
    return pl.pallas_call(
        _ada_kernel,
        grid=(L, N // tn),
        in_specs=[
            pl.BlockSpec((rows, D), lambda l, j: (0, 0)),
            pl.BlockSpec((1, D, R), lambda l, j: (l, 0, 0)),
            pl.BlockSpec((1, R, tn), lambda l, j: (l, 0, j)),
            pl.BlockSpec((1, 1, tn), lambda l, j: (l, 0, j)),
        ],
        out_specs=pl.BlockSpec((1, rows, tn), lambda l, j: (l, 0, j)),
        out_shape=jax.ShapeDtypeStruct((L, rows, N), F32),
        compiler_params=_params("arbitrary", "arbitrary"),
        name="ada_modulation",
    )(cv, w_dn, w_up, b_up.reshape(L, 1, N))


def _modulate_kernel(x_ref, mod_ref, h_ref):
    sh = mod_ref[0, 0:1, :]
    sc = mod_ref[0, 1:2, :]
    h_ref[0] = (x_ref[0] * (1.0 + sc) + sh).astype(BF16)


def modulate_rows(x, mod):
    B, T, D = x.shape
    tr = _tile(T, 512)
    per_batch = mod.shape[0] > 1
    return pl.pallas_call(
        _modulate_kernel,
        grid=(B, T // tr),
        in_specs=[
            pl.BlockSpec((1, tr, D), lambda b, i: (b, i, 0)),
            pl.BlockSpec((1, 6, D), (lambda b, i: (b, 0, 0)) if per_batch else (lambda b, i: (0, 0, 0))),
        ],
        out_specs=pl.BlockSpec((1, tr, D), lambda b, i: (b, i, 0)),
        out_shape=jax.ShapeDtypeStruct((B, T, D), BF16),
        compiler_params=_params("arbitrary", "arbitrary"),
        name="modulate",
    )(x, mod)


def _matmul_kernel(a_ref, w_ref, o_ref):
    o_ref[...] = _dot(a_ref[...], w_ref[...]).astype(o_ref.dtype)


def matmul(a, w, out_dtype=BF16):
    M, K = a.shape
    N = w.shape[1]
    tm, tn = _tile(M, 1024), _tile(N, 1024)
    return pl.pallas_call(
        _matmul_kernel,
        grid=(M // tm, N // tn),
        in_specs=[
            pl.BlockSpec((tm, K), lambda i, j: (i, 0)),
            pl.BlockSpec((K, tn), lambda i, j: (0, j)),
        ],
        out_specs=pl.BlockSpec((tm, tn), lambda i, j: (i, j)),
        out_shape=jax.ShapeDtypeStruct((M, N), out_dtype),
        compiler_params=_params("arbitrary", "arbitrary"),
        name="matmul",
    )(a, w)


def _normrope_kernel(*refs, nq, nkv, rope, scale):
    if rope:
        q_ref, kv_ref, qg_ref, kg_ref, cos_ref, sin_ref, qo_ref, ko_ref, vo_ref = refs
        cs = cos_ref[...]
        sn = sin_ref[...]
        lane = lax.broadcasted_iota(jnp.int32, cs.shape, 1)
        even = (lane & 1) == 0
    else:
        q_ref, kv_ref, qg_ref, kg_ref, qo_ref, ko_ref, vo_ref = refs

    def head(xh, g):
        xf = xh.astype(F32)
        y = xf * lax.rsqrt(jnp.mean(xf * xf, axis=-1, keepdims=True) + NORM_EPS) * g
        if rope:
            partner = jnp.where(even, pltpu.roll(y, HEAD_DIM - 1, 1), pltpu.roll(y, 1, 1))
            y = y * cs + partner * sn
        return y

    qg = qg_ref[...]
    kg = kg_ref[...]
    for h in range(nq):
        sl = slice(h * HEAD_DIM, (h + 1) * HEAD_DIM)
        qo_ref[0, :, sl] = (head(q_ref[0, :, sl], qg) * scale).astype(BF16)
    for h in range(nkv):
        sl = slice(h * HEAD_DIM, (h + 1) * HEAD_DIM)
        ko_ref[0, :, sl] = head(kv_ref[0, :, sl], kg).astype(BF16)
    vo_ref[0] = kv_ref[0, :, nkv * HEAD_DIM:]


def norm_rope(proj, q_off, qw, kvw, qg, kg, cos2, sin2):
    B, T, _ = proj.shape
    rope = cos2 is not None
    tr = _tile(T, 512)
    assert q_off % qw == 0 and (q_off + qw) % (2 * kvw) == 0
    qb, kvb = q_off // qw, (q_off + qw) // (2 * kvw)
    in_specs = [
        pl.BlockSpec((1, tr, qw), lambda b, i: (b, i, qb)),
        pl.BlockSpec((1, tr, 2 * kvw), lambda b, i: (b, i, kvb)),
        pl.BlockSpec((1, HEAD_DIM), lambda b, i: (0, 0)),
        pl.BlockSpec((1, HEAD_DIM), lambda b, i: (0, 0)),
    ]
    args = [proj, proj, qg.reshape(1, HEAD_DIM), kg.reshape(1, HEAD_DIM)]
    if rope:
        in_specs += [pl.BlockSpec((tr, HEAD_DIM), lambda b, i: (i, 0))] * 2
        args += [cos2, sin2]
    kern = functools.partial(_normrope_kernel, nq=qw // HEAD_DIM, nkv=kvw // HEAD_DIM, rope=rope,
                             scale=HEAD_DIM ** -0.5)
    return pl.pallas_call(
        kern,
        grid=(B, T // tr),
        in_specs=in_specs,
        out_specs=[
            pl.BlockSpec((1, tr, qw), lambda b, i: (b, i, 0)),
            pl.BlockSpec((1, tr, kvw), lambda b, i: (b, i, 0)),
            pl.BlockSpec((1, tr, kvw), lambda b, i: (b, i, 0)),
        ],
        out_shape=[
            jax.ShapeDtypeStruct((B, T, qw), BF16),
            jax.ShapeDtypeStruct((B, T, kvw), BF16),
            jax.ShapeDtypeStruct((B, T, kvw), BF16),
        ],
        compiler_params=_params("arbitrary", "arbitrary"),
        name="norm_rope",
    )(*args)


def _attn_kernel(q_ref, k_ref, v_ref, o_ref, vx_ref, m_ref, acc_ref, *, tq, first, tk, n_chunks):
    @pl.when(pl.program_id(2) == 0)
    def _():
        vx_ref[:, 0:HEAD_DIM] = v_ref[0]
        vx_ref[:, HEAD_DIM:] = jnp.ones((vx_ref.shape[0], HEAD_DIM), BF16)

    q = jnp.concatenate([q_ref[0, :, g * HEAD_DIM:(g + 1) * HEAD_DIM] for g in range(GQA_GROUP)], axis=0)
    m_ref[...] = jnp.full(m_ref.shape, -jnp.inf, F32)
    acc_ref[...] = jnp.zeros(acc_ref.shape, F32)

    def chunk(off, size):
        k = k_ref[0, pl.ds(off, size), :]
        vx = vx_ref[pl.ds(off, size), :]
        for r in range(0, GQA_GROUP * tq, ATTN_SUB):
            rows = slice(r, r + ATTN_SUB)
            s = lax.dot_general(q[rows], k, (((1,), (1,)), ((), ())), preferred_element_type=F32)
            m_prev = m_ref[rows, :]
            m_next = jnp.maximum(m_prev, jnp.max(s, axis=-1, keepdims=True))
            alpha = jnp.exp(m_prev - m_next)
            p = jnp.exp(s - jnp.concatenate([m_next] * (size // HEAD_DIM), axis=1))
            acc_ref[rows, :] = jnp.concatenate([alpha, alpha], axis=1) * acc_ref[rows, :] + _dot(p.astype(BF16), vx)
            m_ref[rows, :] = m_next

    chunk(0, first)
    for i in range(n_chunks):
        chunk(first + i * tk, tk)
    acc = acc_ref[...]
    o = acc[:, :HEAD_DIM] * (1.0 / acc[:, HEAD_DIM:])
    for g in range(GQA_GROUP):
        o_ref[0, :, g * HEAD_DIM:(g + 1) * HEAD_DIM] = o[g * tq:(g + 1) * tq].astype(BF16)


def attention(q, k, v):
    B, Tq, QW = q.shape
    Tk, KVW = k.shape[1], k.shape[2]
    nkv = KVW // HEAD_DIM
    tq = _tile(Tq, ATTN_TQ)
    tk = min(ATTN_TK, Tk)
    n_chunks = (Tk - 1) // tk
    first = Tk - n_chunks * tk
    assert first % HEAD_DIM == 0 and tk % HEAD_DIM == 0
    gw = GQA_GROUP * HEAD_DIM
    kern = functools.partial(_attn_kernel, tq=tq, first=first, tk=tk, n_chunks=n_chunks)
    return pl.pallas_call(
        kern,
        grid=(B, nkv, Tq // tq),
        in_specs=[
            pl.BlockSpec((1, tq, gw), lambda b, h, i: (b, i, h)),
            pl.BlockSpec((1, Tk, HEAD_DIM), lambda b, h, i: (b, 0, h)),
            pl.BlockSpec((1, Tk, HEAD_DIM), lambda b, h, i: (b, 0, h)),
        ],
        out_specs=pl.BlockSpec((1, tq, gw), lambda b, h, i: (b, i, h)),
        out_shape=jax.ShapeDtypeStruct((B, Tq, QW), BF16),
        scratch_shapes=[
            pltpu.VMEM((Tk, 2 * HEAD_DIM), BF16),
            pltpu.VMEM((GQA_GROUP * tq, HEAD_DIM), F32),
            pltpu.VMEM((GQA_GROUP * tq, 2 * HEAD_DIM), F32),
        ],
        compiler_params=_params("arbitrary", "arbitrary", "arbitrary"),
        name="attention",
    )(q, k, v)


def _gelu_tanh(x):
    return 0.5 * x * (1.0 + jnp.tanh(0.7978845608028654 * (x + 0.044715 * x * x * x)))


def _lru_kernel(xl_ref, gl_ref, xc_ref, gc_ref, cw_ref, cb_ref, wr_ref, br_ref, wi_ref, bi_ref, lam_ref,
                yl_ref, yc_ref, pad_ref, al_ref, bl_ref, hl_ref, pl_ref, ac_ref, bc_ref, hc_ref, pc_ref,
                *, Ll, Lc):
    S = V7X_SUBLANES
    cw = cw_ref[...]
    cb = cb_ref[...]

    def coeffs(x_ref, L, a_ref, b_ref):
        pad_ref[0:S, :] = jnp.zeros((S, LRU_BLOCK), F32)
        pad_ref[S:S + L, :] = x_ref[0].astype(F32)
        pad_ref[S + L:2 * S + L, :] = jnp.zeros((S, LRU_BLOCK), F32)
        u = cb
        for j in range(CONV_W):
            start = S - CONV_LEFT + j
            u = u + pad_ref[start:start + L, :] * cw[j:j + 1, :]
        ub = u.astype(BF16)
        for d in range(2):
            r = _sigmoid(_dot(ub, wr_ref[d, 0]) + br_ref[d:d + 1, :])
            i = _sigmoid(_dot(ub, wi_ref[d, 0]) + bi_ref[d:d + 1, :])
            nl = -lam_ref[d:d + 1, :]
            softplus = jnp.maximum(nl, 0.0) + jnp.log(1.0 + jnp.exp(-jnp.abs(nl)))
            log_a = (-LRU_C) * r * softplus
            a = jnp.exp(log_a)
            a_ref[d] = a
            b_ref[d] = jnp.sqrt(1.0 - a * a) * i * u

    def local_scan(L, a_ref, b_ref, h_ref, p_ref):
        n = L // S

        def body(s, carry):
            hf, pf, hr, pr = carry
            af = a_ref[0, pl.ds(s, S, stride=n), :]
            bf = b_ref[0, pl.ds(s, S, stride=n), :]
            hf = af * hf + bf
            pf = af * pf
            h_ref[0, pl.ds(s, S, stride=n), :] = hf
            p_ref[0, pl.ds(s, S, stride=n), :] = pf
            t = n - 1 - s
            ar = a_ref[1, pl.ds(t, S, stride=n), :]
            brv = b_ref[1, pl.ds(t, S, stride=n), :]
            hr = ar * hr + brv
            pr = ar * pr
            h_ref[1, pl.ds(t, S, stride=n), :] = hr
            p_ref[1, pl.ds(t, S, stride=n), :] = pr
            return hf, pf, hr, pr

        z = jnp.zeros((S, LRU_BLOCK), F32)
        o = jnp.ones((S, LRU_BLOCK), F32)
        return lax.fori_loop(0, n, body, (z, o, z, o))

    def resolve(h_end, p_end, init, reverse):
        order = range(S - 1, -1, -1) if reverse else range(S)
        inits = [None] * S
        cur = init
        for c in order:
            inits[c] = cur
            cur = h_end[c:c + 1, :] + p_end[c:c + 1, :] * cur
        return inits, cur

    def finish(L, h_ref, p_ref, inits_f, inits_r, g_ref, y_ref):
        n = L // S
        for c in range(S):
            rows = slice(c * n, (c + 1) * n)
            hf = h_ref[0, rows, :] + p_ref[0, rows, :] * inits_f[c]
            hr = h_ref[1, rows, :] + p_ref[1, rows, :] * inits_r[c]
            y_ref[0, rows, :] = ((hf + hr) * _gelu_tanh(g_ref[0, rows, :].astype(F32))).astype(BF16)

    coeffs(xc_ref, Lc, ac_ref, bc_ref)
    coeffs(xl_ref, Ll, al_ref, bl_ref)
    hcf, pcf, hcr, pcr = local_scan(Lc, ac_ref, bc_ref, hc_ref, pc_ref)
    hlf, plf, hlr, plr = local_scan(Ll, al_ref, bl_ref, hl_ref, pl_ref)
    zero = jnp.zeros((1, LRU_BLOCK), F32)
    ic_f, end_cf = resolve(hcf, pcf, zero, False)
    ic_r, end_cr = resolve(hcr, pcr, zero, True)
    il_f, _ = resolve(hlf, plf, end_cf, False)
    il_r, _ = resolve(hlr, plr, end_cr, True)
    finish(Lc, hc_ref, pc_ref, ic_f, ic_r, gc_ref, yc_ref)
    finish(Ll, hl_ref, pl_ref, il_f, il_r, gl_ref, yl_ref)


def lru_branch(proj_l, proj_c, lru_w, conv_w, conv_b, w_rg, b_rg, w_ig, b_ig, lam):
    B, Ll, _ = proj_l.shape
    Lc = proj_c.shape[1]
    nb = lru_w // LRU_BLOCK
    S = V7X_SUBLANES
    blk = lambda L, off: pl.BlockSpec((1, L, LRU_BLOCK), lambda b, h: (b, 0, h + off))
    chan = lambda rows: pl.BlockSpec((rows, LRU_BLOCK), lambda b, h: (0, h))
    wspec = pl.BlockSpec((2, 1, LRU_BLOCK, LRU_BLOCK), lambda b, h: (0, h, 0, 0))
    kern = functools.partial(_lru_kernel, Ll=Ll, Lc=Lc)
    return pl.pallas_call(
        kern,
        grid=(B, nb),
        in_specs=[blk(Ll, 0), blk(Ll, nb), blk(Lc, 0), blk(Lc, nb), chan(CONV_W), chan(1),
                  wspec, chan(2), wspec, chan(2), chan(2)],
        out_specs=[pl.BlockSpec((1, Ll, LRU_BLOCK), lambda b, h: (b, 0, h)),
                   pl.BlockSpec((1, Lc, LRU_BLOCK), lambda b, h: (b, 0, h))],
        out_shape=[jax.ShapeDtypeStruct((B, Ll, lru_w), BF16), jax.ShapeDtypeStruct((B, Lc, lru_w), BF16)],
        scratch_shapes=[pltpu.VMEM((Ll + 2 * S, LRU_BLOCK), F32)]
        + [pltpu.VMEM((2, Ll, LRU_BLOCK), F32)] * 4 + [pltpu.VMEM((2, Lc, LRU_BLOCK), F32)] * 4,
        compiler_params=_params("arbitrary", "arbitrary"),
        name="rglru",
    )(proj_l, proj_l, proj_c, proj_c, conv_w, conv_b.reshape(1, lru_w), w_rg, b_rg, w_ig, b_ig, lam)


def _merge_kernel(att_ref, lru_ref, ga_ref, gl_ref, wa_ref, wl_ref, o_ref):
    ya = _dot(att_ref[...], wa_ref[0])
    yl = _dot(lru_ref[...], wl_ref[0])
    m = _sigmoid(ga_ref[...].astype(F32)) * ya + _sigmoid(gl_ref[...].astype(F32)) * yl
    o_ref[...] = m.astype(BF16)


def merge_branches(att, lru, proj, ga_off, D, w_branch):
    M, K = att.shape
    tm = _tile(M, 1024)
    tn = 512
    while ga_off % tn or D % tn:
        tn //= 2
    ga_b, gl_b = ga_off // tn, (ga_off + D) // tn
    return pl.pallas_call(
        _merge_kernel,
        grid=(M // tm, D // tn),
        in_specs=[
            pl.BlockSpec((tm, K), lambda i, j: (i, 0)),
            pl.BlockSpec((tm, K), lambda i, j: (i, 0)),
            pl.BlockSpec((tm, tn), lambda i, j: (i, ga_b + j)),
            pl.BlockSpec((tm, tn), lambda i, j: (i, gl_b + j)),
            pl.BlockSpec((1, K, tn), lambda i, j: (0, 0, j)),
            pl.BlockSpec((1, K, tn), lambda i, j: (1, 0, j)),
        ],
        out_specs=pl.BlockSpec((tm, tn), lambda i, j: (i, j)),
        out_shape=jax.ShapeDtypeStruct((M, D), BF16),
        compiler_params=_params("arbitrary", "arbitrary"),
        name="merge",
    )(att, lru, proj, proj, w_branch, w_branch)


def _postln_kernel(*refs, gate_row, sh_row, sc_row, alpha, router, emit_h):
    x_ref, y_ref, mod_ref, modn_ref, lng_ref, lnb_ref = refs[:6]
    rest = list(refs[6:])
    wr_ref = rest.pop(0) if router else None
    xo_ref = rest.pop(0)
    ho_ref = rest.pop(0) if emit_h else None
    lg_ref = rest.pop(0) if router else None
    z = alpha * x_ref[0] + mod_ref[0, gate_row:gate_row + 1, :] * y_ref[0].astype(F32)
    mu = jnp.mean(z, axis=-1, keepdims=True)
    zc = z - mu
    var = jnp.mean(zc * zc, axis=-1, keepdims=True)
    xn = zc * lax.rsqrt(var + NORM_EPS) * lng_ref[...] + lnb_ref[...]
    xo_ref[0] = xn
    if emit_h:
        h = xn * (1.0 + modn_ref[0, sc_row:sc_row + 1, :]) + modn_ref[0, sh_row:sh_row + 1, :]
        ho_ref[0] = h.astype(BF16)
        if router:
            lg_ref[0] = lax.dot_general(wr_ref[...], h, (((1,), (1,)), ((), ())),
                                        preferred_element_type=F32, precision=HIGHEST)


def post_ln(x, y, mod, modn, ln_g, ln_b, alpha, gate_row, sh_row, sc_row, w_router_t=None, emit_h=True):
    B, T, D = x.shape
    tr = _tile(T, 256)
    router = w_router_t is not None
    mspec = lambda m: pl.BlockSpec((1, 6, D), (lambda b, i: (b, 0, 0)) if m.shape[0] > 1 else (lambda b, i: (0, 0, 0)))
    row = pl.BlockSpec((1, tr, D), lambda b, i: (b, i, 0))
    vec = pl.BlockSpec((1, D), lambda b, i: (0, 0))
    in_specs = [row, row, mspec(mod), mspec(modn), vec, vec]
    args = [x, y, mod, modn, ln_g.reshape(1, D), ln_b.reshape(1, D)]
    out_specs = [row]
    out_shape = [jax.ShapeDtypeStruct((B, T, D), F32)]
    if router:
        E = w_router_t.shape[0]
        in_specs.append(pl.BlockSpec((E, D), lambda b, i: (0, 0)))
        args.append(w_router_t)
    if emit_h:
        out_specs.append(row)
        out_shape.append(jax.ShapeDtypeStruct((B, T, D), BF16))
    if router:
        out_specs.append(pl.BlockSpec((1, E, tr), lambda b, i: (b, 0, i)))
        out_shape.append(jax.ShapeDtypeStruct((B, E, T), F32))
    kern = functools.partial(_postln_kernel, gate_row=gate_row, sh_row=sh_row, sc_row=sc_row, alpha=alpha,
                             router=router, emit_h=emit_h)
    return pl.pallas_call(
        kern,
        grid=(B, T // tr),
        in_specs=in_specs,
        out_specs=out_specs,
        out_shape=out_shape,
        compiler_params=_params("arbitrary", "arbitrary"),
        name="post_ln",
    )(*args)


def _topk_kernel(lg_ref, pos_ref, gate_ref, cnt_ref, *, T, cap, tw):
    lg = lg_ref[0]
    E = lg.shape[0]
    ex = jnp.exp(lg - jnp.max(lg, axis=0, keepdims=True))
    aff = ex / jnp.sum(ex, axis=0, keepdims=True)
    bits = lax.bitcast_convert_type(aff, jnp.int32)

    def count(mask):
        return jnp.sum(jnp.where(mask, 1.0, 0.0), axis=1, keepdims=True)

    thr = jnp.zeros((E, 1), jnp.int32)
    for bit in range(30, -1, -1):
        cand = thr | (1 << bit)
        thr = jnp.where(count(bits >= cand) >= cap, cand, thr)

    t_row = lax.broadcasted_iota(jnp.int32, (T, V7X_LANES), 0)
    w_col = lax.broadcasted_iota(jnp.int32, (T, V7X_LANES), 1)
    before_window = jnp.where(t_row < w_col * tw, 1.0, 0.0).astype(BF16)
    r_i = lax.broadcasted_iota(jnp.int32, (tw, tw), 0)
    c_i = lax.broadcasted_iota(jnp.int32, (tw, tw), 1)
    strict_upper = jnp.where(r_i < c_i, 1.0, 0.0).astype(BF16)

    def prefix(mask):
        mb = jnp.where(mask, 1.0, 0.0).astype(BF16)
        starts = _dot(mb, before_window)
        parts = [_dot(mb[:, w * tw:(w + 1) * tw], strict_upper) + starts[:, w:w + 1] for w in range(T // tw)]
        return jnp.concatenate(parts, axis=1) if len(parts) > 1 else parts[0], starts

    gt = bits > thr
    eq = bits == thr
    need = cap - count(gt)
    tie_rank, _ = prefix(eq)
    sel = gt | (eq & (tie_rank < need))
    slot, starts = prefix(sel)
    pos_ref[0] = jnp.where(sel, slot.astype(jnp.int32), -1)
    gate_ref[0] = jnp.where(sel, aff, 0.0)
    cnt_ref[0] = starts.astype(jnp.int32)


def expert_topk(logits, cap):
    B, E, T = logits.shape
    tw = min(TOKEN_WINDOW, T)
    kern = functools.partial(_topk_kernel, T=T, cap=cap, tw=tw)
    spec = pl.BlockSpec((1, E, T), lambda b: (b, 0, 0))
    return pl.pallas_call(
        kern,
        grid=(B,),
        in_specs=[spec],
        out_specs=[spec, spec, pl.BlockSpec((1, E, V7X_LANES), lambda b: (b, 0, 0))],
        out_shape=[jax.ShapeDtypeStruct((B, E, T), jnp.int32), jax.ShapeDtypeStruct((B, E, T), F32),
                   jax.ShapeDtypeStruct((B, E, V7X_LANES), jnp.int32)],
        compiler_params=_params("arbitrary"),
        name="expert_topk",
    )(logits)


def _slot_span(cnt_ref, idx, C, sp):
    lo = cnt_ref[idx]
    hi = cnt_ref[idx + 1]
    start = jnp.minimum((lo // BF16_ROWS) * BF16_ROWS, C - sp)
    n_extra = jnp.maximum(hi - start - 1, 0) // sp
    return pl.multiple_of(start, BF16_ROWS), n_extra


def _gather_kernel(cnt_ref, pos_ref, h_ref, xs_ref, acc_ref, *, C, sp, nw, E):
    b = pl.program_id(0)
    w = pl.program_id(2)

    @pl.when(w == 0)
    def _():
        acc_ref[...] = jnp.zeros(acc_ref.shape, F32)

    hwin = h_ref[0]
    tw = hwin.shape[0]
    slot_i = lax.broadcasted_iota(jnp.int32, (sp, tw), 0)
    spans = [_slot_span(cnt_ref, (b * E + e) * (nw + 1) + w, C, sp) for e in range(E)]
    onehots = [jnp.where(pos_ref[0, e:e + 1, :] == slot_i + spans[e][0], 1.0, 0.0).astype(BF16)
               for e in range(E)]
    rows = _dot(jnp.concatenate(onehots, axis=0), hwin)
    for e in range(E):
        start, n_extra = spans[e]
        acc_ref[e, pl.ds(start, sp), :] += rows[e * sp:(e + 1) * sp]

        def extra(i, carry):
            j0 = start + sp * (i + 1)
            j0c = pl.multiple_of(jnp.minimum(j0, C - sp), BF16_ROWS)
            slot = slot_i + j0c
            hit = (pos_ref[0, e:e + 1, :] == slot) & (slot >= j0)
            acc_ref[e, pl.ds(j0c, sp), :] += _dot(jnp.where(hit, 1.0, 0.0).astype(BF16), hwin)
            return carry

        lax.fori_loop(0, n_extra, extra, 0)

    @pl.when(w == nw - 1)
    def _():
        xs_ref[0] = acc_ref[...].astype(BF16)


def gather_tokens(h, pos, cnt_flat, C):
    B, T, D = h.shape
    E = pos.shape[1]
    tw = min(TOKEN_WINDOW, T)
    nw = T // tw
    sp = min(SLOT_SPAN, C)
    dc = _tile(D, 512)
    kern = functools.partial(_gather_kernel, C=C, sp=sp, nw=nw, E=E)
    grid_spec = pltpu.PrefetchScalarGridSpec(
        num_scalar_prefetch=1,
        grid=(B, D // dc, nw),
        in_specs=[
            pl.BlockSpec((1, E, tw), lambda b, d, w, cnt: (b, 0, w)),
            pl.BlockSpec((1, tw, dc), lambda b, d, w, cnt: (b, w, d)),
        ],
        out_specs=pl.BlockSpec((1, E, C, dc), lambda b, d, w, cnt: (b, 0, 0, d)),
        scratch_shapes=[pltpu.VMEM((E, C, dc), F32)],
    )
    return pl.pallas_call(
        kern,
        grid_spec=grid_spec,
        out_shape=jax.ShapeDtypeStruct((B, E, C, D), BF16),
        compiler_params=_params("arbitrary", "arbitrary", "arbitrary"),
        name="moe_gather",
    )(cnt_flat, pos, h)


def _ffn_kernel(xs_ref, wgu_ref, wd_ref, y_ref, *, F):
    gu = _dot(xs_ref[0, 0], wgu_ref[0])
    g = gu[:, :F]
    u = gu[:, F:]
    act = (g * _sigmoid(g) * u).astype(BF16)
    y_ref[0, 0] = _dot(act, wd_ref[0]).astype(BF16)


def expert_ffn(xs, w_gate_up, w_down):
    B, E, C, D = xs.shape
    F = w_down.shape[1]
    kern = functools.partial(_ffn_kernel, F=F)
    return pl.pallas_call(
        kern,
        grid=(E, B),
        in_specs=[
            pl.BlockSpec((1, 1, C, D), lambda e, b: (b, e, 0, 0)),
            pl.BlockSpec((1, D, 2 * F), lambda e, b: (e, 0, 0)),
            pl.BlockSpec((1, F, D), lambda e, b: (e, 0, 0)),
        ],
        out_specs=pl.BlockSpec((1, 1, C, D), lambda e, b: (b, e, 0, 0)),
        out_shape=jax.ShapeDtypeStruct((B, E, C, D), BF16),
        compiler_params=_params("arbitrary", "arbitrary"),
        name="expert_ffn",
    )(xs, w_gate_up, w_down)


def _scatter_kernel(cnt_ref, post_ref, gatet_ref, y_ref, o_ref, acc_ref, *, C, sp, nw, E):
    b = pl.program_id(0)
    w = pl.program_id(2)
    tw = post_ref.shape[1]
    lane = lax.broadcasted_iota(jnp.int32, (tw, 2 * sp), 1)
    lane_slot = lane & (sp - 1)
    first_half = lane < sp

    def gated_onehot(e, j0c, j0):
        pcol = post_ref[0, :, e:e + 1]
        g = gatet_ref[0, :, e:e + 1]
        g_hi = g.astype(BF16).astype(F32)
        slot = lane_slot + j0c
        hit = pcol == slot
        if j0 is not None:
            hit = hit & (slot >= j0)
        return jnp.where(hit, jnp.where(first_half, g_hi, g - g_hi), 0.0).astype(BF16)

    def contribution(experts, starts, j0):
        lhs = [gated_onehot(e, s, j0) for e, s in zip(experts, starts)]
        rhs = []
        for e, s in zip(experts, starts):
            ysp = y_ref[0, e, pl.ds(s, sp), :]
            rhs += [ysp, ysp]
        return _dot(lhs[0] if len(lhs) == 1 else jnp.concatenate(lhs, axis=1), jnp.concatenate(rhs, axis=0))

    spans = [_slot_span(cnt_ref, (b * E + e) * (nw + 1) + w, C, sp) for e in range(E)]
    total = None
    for e in range(0, E, 2):
        r = contribution((e, e + 1), (spans[e][0], spans[e + 1][0]), None)
        total = r if total is None else total + r
    acc_ref[...] = total
    for e in range(E):
        start, n_extra = spans[e]

        def extra(i, carry):
            j0 = start + sp * (i + 1)
            j0c = pl.multiple_of(jnp.minimum(j0, C - sp), BF16_ROWS)
            acc_ref[...] += contribution((e,), (j0c,), j0)
            return carry

        lax.fori_loop(0, n_extra, extra, 0)
    o_ref[0] = acc_ref[...].astype(o_ref.dtype)


def scatter_tokens(y, pos_t, gate_t, cnt_flat):
    B, E, C, D = y.shape
    T = pos_t.shape[1]
    tw = min(TOKEN_WINDOW, T)
    nw = T // tw
    sp = min(SLOT_SPAN, C)
    dc = _tile(D, 1024)
    kern = functools.partial(_scatter_kernel, C=C, sp=sp, nw=nw, E=E)
    grid_spec = pltpu.PrefetchScalarGridSpec(
        num_scalar_prefetch=1,
        grid=(B, D // dc, nw),
        in_specs=[
            pl.BlockSpec((1, tw, E), lambda b, d, w, cnt: (b, w, 0)),
            pl.BlockSpec((1, tw, E), lambda b, d, w, cnt: (b, w, 0)),
            pl.BlockSpec((1, E, C, dc), lambda b, d, w, cnt: (b, 0, 0, d)),
        ],
        out_specs=pl.BlockSpec((1, tw, dc), lambda b, d, w, cnt: (b, w, d)),
        scratch_shapes=[pltpu.VMEM((tw, dc), F32)],
    )
    return pl.pallas_call(
        kern,
        grid_spec=grid_spec,
        out_shape=jax.ShapeDtypeStruct((B, T, D), BF16),
        compiler_params=_params("arbitrary", "arbitrary", "arbitrary"),
        name="moe_scatter",
    )(cnt_flat, pos_t, gate_t, y)


def expert_choice_ffn(h, logits, w_gate_up, w_down):
    B, T, D = h.shape
    E = logits.shape[1]
    cap = CAP_FACTOR * T // E
    nw = T // min(TOKEN_WINDOW, T)
    pos, gate, cnt = expert_topk(logits, cap)
    cnt_flat = cnt[:, :, :nw + 1].reshape(-1)
    xs = gather_tokens(h, pos, cnt_flat, cap)
    y = expert_ffn(xs, w_gate_up, w_down)
    return scatter_tokens(y, jnp.swapaxes(pos, 1, 2), jnp.swapaxes(gate, 1, 2), cnt_flat)


def _rope_tables(T):
    rows = T // GRID_W
    row = jnp.repeat(jnp.arange(rows, dtype=F32), GRID_W)
    col = jnp.tile(jnp.arange(GRID_W, dtype=F32), rows)
    n_freq = HEAD_DIM // 4
    inv = ROPE_THETA ** (-jnp.arange(n_freq, dtype=F32) / n_freq)
    ang = jnp.concatenate([row[:, None] * inv, col[:, None] * inv], axis=-1)
    cos2 = jnp.repeat(jnp.cos(ang), 2, axis=-1)
    sin = jnp.sin(ang)
    sin2 = jnp.stack([-sin, sin], axis=-1).reshape(T, HEAD_DIM)
    return cos2, sin2


def kernel(x, c, ctx, c_ctx, w_ada_dn, w_ada_up, b_ada, w_in, q_norm_g, k_norm_g, conv_w, conv_b, w_rg, b_rg, w_ig, b_ig, lru_lambda, w_branch, w_out, ln_g, ln_b, w_router, w_gate_up, w_down):
    B, T, D = x.shape
    Tc = ctx.shape[1]
    depth = w_in.shape[0]
    in_w = w_in.shape[2]
    lru_w = conv_w.shape[2]
    kvw = (in_w - 2 * lru_w - 2 * D) // (GQA_GROUP + 2)
    qw = GQA_GROUP * kvw
    q_off = 2 * lru_w
    ga_off = q_off + qw + 2 * kvw
    alpha = (2.0 * depth) ** 0.25

    n_rows = -(-(B + 1) // V7X_SUBLANES) * V7X_SUBLANES
    cv = jnp.zeros((n_rows, D), F32).at[:B].set(c).at[B].set(c_ctx)
    mods = ada_modulation(cv, w_ada_dn, w_ada_up, b_ada).reshape(depth, n_rows, 6, D)
    mod_l = [mods[l, :B] for l in range(depth)]
    mod_c = [mods[l, B:B + 1] for l in range(depth)]

    cos2, sin2 = _rope_tables(T)
    w_in_b = w_in.astype(BF16)
    w_branch_b = w_branch.astype(BF16)
    w_out_b = w_out.astype(BF16)
    w_rg_b = w_rg.astype(BF16)
    w_ig_b = w_ig.astype(BF16)
    w_gu_b = w_gate_up.astype(BF16)
    w_dn_b = w_down.astype(BF16)
    w_router_t = jnp.swapaxes(w_router, 1, 2)

    xl, xc = x, ctx
    hl = modulate_rows(xl, mod_l[0])
    hc = modulate_rows(xc, mod_c[0])
    for l in range(depth):
        need_ctx = l < depth - 1
        last = l == depth - 1
        proj_l = matmul(hl.reshape(B * T, D), w_in_b[l]).reshape(B, T, in_w)
        proj_c = matmul(hc.reshape(B * Tc, D), w_in_b[l]).reshape(B, Tc, in_w)
        q_l, k_l, v_l = norm_rope(proj_l, q_off, qw, kvw, q_norm_g[l], k_norm_g[l], cos2, sin2)
        q_c, k_c, v_c = norm_rope(proj_c, q_off, qw, kvw, q_norm_g[l], k_norm_g[l], None, None)
        k_all = jnp.concatenate([k_c, k_l], axis=1)
        v_all = jnp.concatenate([v_c, v_l], axis=1)
        att_l = attention(q_l, k_all, v_all)
        lru_l, lru_c = lru_branch(proj_l, proj_c, lru_w, conv_w[l], conv_b[l], w_rg_b[l], b_rg[l], w_ig_b[l],
                                  b_ig[l], lru_lambda[l])

        def tail(xs, hs_proj, att, lru, mod, mod_next, Ts):
            m = merge_branches(att.reshape(B * Ts, qw), lru.reshape(B * Ts, lru_w), hs_proj.reshape(B * Ts, in_w),
                               ga_off, D, w_branch_b[l])
            mix = matmul(m, w_out_b[l]).reshape(B, Ts, D)
            x1, h2, logits = post_ln(xs, mix, mod, mod, ln_g[l, 0], ln_b[l, 0], alpha, 2, 3, 4, w_router_t[l])
            moe = expert_choice_ffn(h2, logits, w_gu_b[l], w_dn_b[l])
            if last:
                (x2,) = post_ln(x1, moe, mod, mod, ln_g[l, 1], ln_b[l, 1], alpha, 5, 0, 1, emit_h=False)
                return x2, None
            return post_ln(x1, moe, mod, mod_next, ln_g[l, 1], ln_b[l, 1], alpha, 5, 0, 1)

        xl, hl = tail(xl, proj_l, att_l, lru_l, mod_l[l], None if last else mod_l[l + 1], T)
        if need_ctx:
            att_c = attention(q_c, k_c, v_c)
            xc, hc = tail(xc, proj_c, att_c, lru_c, mod_c[l], mod_c[l + 1], Tc)
    return xl
```

```python
import functools
import math

import jax
import jax.numpy as jnp
from jax import lax
from jax.experimental import pallas as pl
from jax.experimental.pallas import tpu as pltpu

HEAD_DIM = 128
GQA_GROUP = 4
GRID_W = 64
ROPE_THETA = 10000.0
LRU_BLOCK = 128
LRU_C = 8.0
CONV_W = 4
CONV_LEFT = 2
N_EXPERTS = 16
CAP_FACTOR = 2
NORM_EPS = 1e-6
LOG2_E = 1.4426950408889634
TINY_F32 = 1e-30
LRU_SCAN_UNROLL = 4

V7X_LANES = 128
V7X_SUBLANES = 8
V7X_VMEM_LIMIT_BYTES = 56 * 1024 * 1024

ATTN_TQ = 256
ATTN_TK = 1024
ATTN_SUB = 128
TOKEN_WINDOW = 256
SLOT_SPAN = 64
BF16_ROWS = 16

F32 = jnp.float32
BF16 = jnp.bfloat16
HIGHEST = lax.Precision.HIGHEST


def _tile(n, pref):
    t = min(pref, n)
    while n % t:
        t //= 2
    return t


def _params(*sem):
    return pltpu.CompilerParams(dimension_semantics=sem, vmem_limit_bytes=V7X_VMEM_LIMIT_BYTES)


def _sigmoid(x):
    return 1.0 / (1.0 + jnp.exp(-x))


def _dot(a, b):
    return jnp.dot(a, b, preferred_element_type=F32)


def _ada_kernel(c_ref, wdn_ref, wup_ref, b_ref, o_ref):
    c = c_ref[...]
    s = c * _sigmoid(c)
    t = jnp.dot(s, wdn_ref[0], preferred_element_type=F32, precision=HIGHEST)
    o_ref[0] = jnp.dot(t, wup_ref[0], preferred_element_type=F32, precision=HIGHEST) + b_ref[0]


def ada_modulation(cv, w_dn, w_up, b_up):
    L, D, R = w_dn.shape
    N = w_up.shape[2]
    rows = cv.shape[0]
    tn = _tile(N, 4096)
    return pl.pallas_call(
        _ada_kernel,
        grid=(L, N // tn),
        in_specs=[
            pl.BlockSpec((rows, D), lambda l, j: (0, 0)),
            pl.BlockSpec((1, D, R), lambda l, j: (l, 0, 0)),
            pl.BlockSpec((1, R, tn), lambda l, j: (l, 0, j)),
            pl.BlockSpec((1, 1, tn), lambda l, j: (l, 0, j)),
        ],
        out_specs=pl.BlockSpec((1, rows, tn), lambda l, j: (l, 0, j)),
        out_shape=jax.ShapeDtypeStruct((L, rows, N), F32),
        compiler_params=_params("arbitrary", "arbitrary"),
        name="ada_modulation",
    )(cv, w_dn, w_up, b_up.reshape(L, 1, N))


def _modulate_kernel(x_ref, mod_ref, h_ref):
    sh = mod_ref[0, 0:1, :]
    sc = mod_ref[0, 1:2, :]
    h_ref[0] = (x_ref[0] * (1.0 + sc) + sh).astype(BF16)


def modulate_rows(x, mod):
    B, T, D = x.shape
    tr = _tile(T, 512)
    per_batch = mod.shape[0] > 1
    return pl.pallas_call(
        _modulate_kernel,
        grid=(B, T // tr),
        in_specs=[
            pl.BlockSpec((1, tr, D), lambda b, i: (b, i, 0)),
            pl.BlockSpec((1, 6, D), (lambda b, i: (b, 0, 0)) if per_batch else (lambda b, i: (0, 0, 0))),
        ],
        out_specs=pl.BlockSpec((1, tr, D), lambda b, i: (b, i, 0)),
        out_shape=jax.ShapeDtypeStruct((B, T, D), BF16),
        compiler_params=_params("arbitrary", "arbitrary"),
        name="modulate",
    )(x, mod)


def _matmul_kernel(a_ref, w_ref, o_ref):
    o_ref[...] = _dot(a_ref[...], w_ref[0]).astype(o_ref.dtype)


def matmul(a, w, l, out_dtype=BF16):
    M, K = a.shape
    N = w.shape[2]
    tm, tn = _tile(M, 1024), _tile(N, 1024)
    return pl.pallas_call(
        _matmul_kernel,
        grid=(M // tm, N // tn),
        in_specs=[
            pl.BlockSpec((tm, K), lambda i, j: (i, 0)),
            pl.BlockSpec((1, K, tn), lambda i, j: (l, 0, j)),
        ],
        out_specs=pl.BlockSpec((tm, tn), lambda i, j: (i, j)),
        out_shape=jax.ShapeDtypeStruct((M, N), out_dtype),
        compiler_params=_params("arbitrary", "arbitrary"),
        name="matmul",
    )(a, w)


def _normrope_kernel(*refs, nq, nkv, rope, scale):
    if rope:
        q_ref, kv_ref, qg_ref, kg_ref, cos_ref, sin_ref, qo_ref, ko_ref, vo_ref = refs
        cs = cos_ref[...]
        sn = sin_ref[...]
        lane = lax.broadcasted_iota(jnp.int32, cs.shape, 1)
        even = (lane & 1) == 0
    else:
        q_ref, kv_ref, qg_ref, kg_ref, qo_ref, ko_ref, vo_ref = refs

    def head(xh, g):
        xf = xh.astype(F32)
        y = xf * lax.rsqrt(jnp.mean(xf * xf, axis=-1, keepdims=True) + NORM_EPS) * g
        if rope:
            partner = jnp.where(even, pltpu.roll(y, HEAD_DIM - 1, 1), pltpu.roll(y, 1, 1))
            y = y * cs + partner * sn
        return y

    qg = qg_ref[...]
    kg = kg_ref[...]
    for h in range(nq):
        sl = slice(h * HEAD_DIM, (h + 1) * HEAD_DIM)
        qo_ref[0, :, sl] = (head(q_ref[0, :, sl], qg) * scale).astype(BF16)
    for h in range(nkv):
        sl = slice(h * HEAD_DIM, (h + 1) * HEAD_DIM)
        ko_ref[0, :, sl] = head(kv_ref[0, :, sl], kg).astype(BF16)
    vo_ref[0] = kv_ref[0, :, nkv * HEAD_DIM:]


def norm_rope(proj, q_off, qw, kvw, qg, kg, cos2, sin2):
    B, T, _ = proj.shape
    rope = cos2 is not None
    tr = _tile(T, 512)
    assert q_off % qw == 0 and (q_off + qw) % (2 * kvw) == 0
    qb, kvb = q_off // qw, (q_off + qw) // (2 * kvw)
    in_specs = [
        pl.BlockSpec((1, tr, qw), lambda b, i: (b, i, qb)),
        pl.BlockSpec((1, tr, 2 * kvw), lambda b, i: (b, i, kvb)),
        pl.BlockSpec((1, HEAD_DIM), lambda b, i: (0, 0)),
        pl.BlockSpec((1, HEAD_DIM), lambda b, i: (0, 0)),
    ]
    args = [proj, proj, qg.reshape(1, HEAD_DIM), kg.reshape(1, HEAD_DIM)]
    if rope:
        in_specs += [pl.BlockSpec((tr, HEAD_DIM), lambda b, i: (i, 0))] * 2
        args += [cos2, sin2]
    kern = functools.partial(_normrope_kernel, nq=qw // HEAD_DIM, nkv=kvw // HEAD_DIM, rope=rope,
                             scale=HEAD_DIM ** -0.5)
    return pl.pallas_call(
        kern,
        grid=(B, T // tr),
        in_specs=in_specs,
        out_specs=[
            pl.BlockSpec((1, tr, qw), lambda b, i: (b, i, 0)),
            pl.BlockSpec((1, tr, kvw), lambda b, i: (b, i, 0)),
            pl.BlockSpec((1, tr, kvw), lambda b, i: (b, i, 0)),
        ],
        out_shape=[
            jax.ShapeDtypeStruct((B, T, qw), BF16),
            jax.ShapeDtypeStruct((B, T, kvw), BF16),
            jax.ShapeDtypeStruct((B, T, kvw), BF16),
        ],
        compiler_params=_params("arbitrary", "arbitrary"),
        name="norm_rope",
    )(*args)


def _attn_kernel(q_ref, k_ref, v_ref, o_ref, vx_ref, m_ref, acc_ref, *, tq, first, tk, n_chunks):
    @pl.when(pl.program_id(2) == 0)
    def _():
        vx_ref[:, 0:HEAD_DIM] = v_ref[0]
        vx_ref[:, HEAD_DIM:] = jnp.ones((vx_ref.shape[0], HEAD_DIM), BF16)

    q = jnp.concatenate([q_ref[0, :, g * HEAD_DIM:(g + 1) * HEAD_DIM] for g in range(GQA_GROUP)], axis=0)
    m_ref[...] = jnp.full(m_ref.shape, -jnp.inf, F32)
    acc_ref[...] = jnp.zeros(acc_ref.shape, F32)

    def chunk(off, size):
        k = k_ref[0, pl.ds(off, size), :]
        vx = vx_ref[pl.ds(off, size), :]
        for r in range(0, GQA_GROUP * tq, ATTN_SUB):
            rows = slice(r, r + ATTN_SUB)
            s = lax.dot_general(q[rows], k, (((1,), (1,)), ((), ())), preferred_element_type=F32)
            m_prev = m_ref[rows, :]
            m_next = jnp.maximum(m_prev, jnp.max(s, axis=-1, keepdims=True))
            alpha = jnp.exp(m_prev - m_next)
            p = jnp.exp(s - jnp.concatenate([m_next] * (size // HEAD_DIM), axis=1))
            acc_ref[rows, :] = jnp.concatenate([alpha, alpha], axis=1) * acc_ref[rows, :] + _dot(p.astype(BF16), vx)
            m_ref[rows, :] = m_next

    chunk(0, first)
    for i in range(n_chunks):
        chunk(first + i * tk, tk)
    acc = acc_ref[...]
    o = acc[:, :HEAD_DIM] * (1.0 / acc[:, HEAD_DIM:])
    for g in range(GQA_GROUP):
        o_ref[0, :, g * HEAD_DIM:(g + 1) * HEAD_DIM] = o[g * tq:(g + 1) * tq].astype(BF16)


def attention(q, k, v):
    B, Tq, QW = q.shape
    Tk, KVW = k.shape[1], k.shape[2]
    nkv = KVW // HEAD_DIM
    tq = _tile(Tq, ATTN_TQ)
    tk = min(ATTN_TK, Tk)
    n_chunks = (Tk - 1) // tk
    first = Tk - n_chunks * tk
    assert first % HEAD_DIM == 0 and tk % HEAD_DIM == 0
    gw = GQA_GROUP * HEAD_DIM
    kern = functools.partial(_attn_kernel, tq=tq, first=first, tk=tk, n_chunks=n_chunks)
    return pl.pallas_call(
        kern,
        grid=(B, nkv, Tq // tq),
        in_specs=[
            pl.BlockSpec((1, tq, gw), lambda b, h, i: (b, i, h)),
            pl.BlockSpec((1, Tk, HEAD_DIM), lambda b, h, i: (b, 0, h)),
            pl.BlockSpec((1, Tk, HEAD_DIM), lambda b, h, i: (b, 0, h)),
        ],
        out_specs=pl.BlockSpec((1, tq, gw), lambda b, h, i: (b, i, h)),
        out_shape=jax.ShapeDtypeStruct((B, Tq, QW), BF16),
        scratch_shapes=[
            pltpu.VMEM((Tk, 2 * HEAD_DIM), BF16),
            pltpu.VMEM((GQA_GROUP * tq, HEAD_DIM), F32),
            pltpu.VMEM((GQA_GROUP * tq, 2 * HEAD_DIM), F32),
        ],
        compiler_params=_params("arbitrary", "arbitrary", "arbitrary"),
        name="attention",
    )(q, k, v)


def _gelu_tanh(x):
    return 0.5 * x * (1.0 + jnp.tanh(0.7978845608028654 * (x + 0.044715 * x * x * x)))


def _lru_kernel(xl_ref, gl_ref, xc_ref, gc_ref, cw_ref, cb_ref, wr_ref, br_ref, wi_ref, bi_ref, lam_ref,
                yl_ref, yc_ref, pad_ref, al_ref, bl_ref, hl_ref, ac_ref, bc_ref, hc_ref, *, Ll, Lc):
    S = V7X_SUBLANES
    cw = cw_ref[0]
    cb = cb_ref[0]

    def coeffs(x_ref, L, a_ref, b_ref):
        pad_ref[0:S, :] = jnp.zeros((S, LRU_BLOCK), F32)
        pad_ref[S:S + L, :] = x_ref[0].astype(F32)
        pad_ref[S + L:2 * S + L, :] = jnp.zeros((S, LRU_BLOCK), F32)
        u = cb
        for j in range(CONV_W):
            start = S - CONV_LEFT + j
            u = u + pad_ref[start:start + L, :] * cw[j:j + 1, :]
        ub = u.astype(BF16)
        for d in range(2):
            r = _sigmoid(_dot(ub, wr_ref[0, d, 0]) + br_ref[0, d:d + 1, :])
            i = _sigmoid(_dot(ub, wi_ref[0, d, 0]) + bi_ref[0, d:d + 1, :])
            nl = -lam_ref[0, d:d + 1, :]
            softplus = jnp.maximum(nl, 0.0) + jnp.log(1.0 + jnp.exp(-jnp.abs(nl)))
            a = jnp.exp2(r * ((-LRU_C * LOG2_E) * softplus))
            a_ref[d] = a
            gap = 1.0 - a * a
            b_ref[d] = gap * lax.rsqrt(jnp.maximum(gap, TINY_F32)) * i * u

    row = lax.broadcasted_iota(jnp.int32, (S, LRU_BLOCK), 0)

    def tile_scan(a, b, carry, reverse):
        for d in (1, 2, 4):
            if reverse:
                keep = row < S - d
                shift = S - d
            else:
                keep = row >= d
                shift = d
            b = b + a * jnp.where(keep, pltpu.roll(b, shift, 0), 0.0)
            a = a * jnp.where(keep, pltpu.roll(a, shift, 0), 1.0)
        h = b + a * carry
        last = h[0:1, :] if reverse else h[S - 1:S, :]
        return h, jnp.broadcast_to(last, (S, LRU_BLOCK))

    def scan(L, a_ref, b_ref, h_ref, carry_f, carry_r):
        n = L // S

        def body(s, carry):
            cf, cr = carry
            for j in range(LRU_SCAN_UNROLL):
                rf = pl.ds(pl.multiple_of((s * LRU_SCAN_UNROLL + j) * S, S), S)
                hf, cf = tile_scan(a_ref[0, rf, :], b_ref[0, rf, :], cf, False)
                h_ref[0, rf, :] = hf
                rr = pl.ds(pl.multiple_of((n - 1 - s * LRU_SCAN_UNROLL - j) * S, S), S)
                hr, cr = tile_scan(a_ref[1, rr, :], b_ref[1, rr, :], cr, True)
                h_ref[1, rr, :] = hr
            return cf, cr

        return lax.fori_loop(0, n // LRU_SCAN_UNROLL, body, (carry_f, carry_r))

    def finish(h_ref, g_ref, y_ref):
        y_ref[0] = ((h_ref[0] + h_ref[1]) * _gelu_tanh(g_ref[0].astype(F32))).astype(BF16)

    coeffs(xc_ref, Lc, ac_ref, bc_ref)
    coeffs(xl_ref, Ll, al_ref, bl_ref)
    zero = jnp.zeros((S, LRU_BLOCK), F32)
    end_f, end_r = scan(Lc, ac_ref, bc_ref, hc_ref, zero, zero)
    scan(Ll, al_ref, bl_ref, hl_ref, end_f, end_r)
    finish(hc_ref, gc_ref, yc_ref)
    finish(hl_ref, gl_ref, yl_ref)


def lru_branch(proj_l, proj_c, lru_w, l, conv_w, conv_b, w_rg, b_rg, w_ig, b_ig, lam):
    B, Ll, _ = proj_l.shape
    Lc = proj_c.shape[1]
    nb = lru_w // LRU_BLOCK
    S = V7X_SUBLANES
    assert Ll % (S * LRU_SCAN_UNROLL) == 0 and Lc % (S * LRU_SCAN_UNROLL) == 0
    blk = lambda L, off: pl.BlockSpec((1, L, LRU_BLOCK), lambda b, h: (b, 0, h + off))
    chan = lambda rows: pl.BlockSpec((1, rows, LRU_BLOCK), lambda b, h: (l, 0, h))
    wspec = pl.BlockSpec((1, 2, 1, LRU_BLOCK, LRU_BLOCK), lambda b, h: (l, 0, h, 0, 0))
    kern = functools.partial(_lru_kernel, Ll=Ll, Lc=Lc)
    return pl.pallas_call(
        kern,
        grid=(B, nb),
        in_specs=[blk(Ll, 0), blk(Ll, nb), blk(Lc, 0), blk(Lc, nb), chan(CONV_W), chan(1),
                  wspec, chan(2), wspec, chan(2), chan(2)],
        out_specs=[pl.BlockSpec((1, Ll, LRU_BLOCK), lambda b, h: (b, 0, h)),
                   pl.BlockSpec((1, Lc, LRU_BLOCK), lambda b, h: (b, 0, h))],
        out_shape=[jax.ShapeDtypeStruct((B, Ll, lru_w), BF16), jax.ShapeDtypeStruct((B, Lc, lru_w), BF16)],
        scratch_shapes=[pltpu.VMEM((Ll + 2 * S, LRU_BLOCK), F32)]
        + [pltpu.VMEM((2, Ll, LRU_BLOCK), F32)] * 3 + [pltpu.VMEM((2, Lc, LRU_BLOCK), F32)] * 3,
        compiler_params=_params("arbitrary", "arbitrary"),
        name="rglru",
    )(proj_l, proj_l, proj_c, proj_c, conv_w, conv_b.reshape(conv_b.shape[0], 1, lru_w), w_rg, b_rg, w_ig, b_ig, lam)


def _merge_kernel(att_ref, lru_ref, ga_ref, gl_ref, wa_ref, wl_ref, o_ref):
    ya = _dot(att_ref[...], wa_ref[0, 0])
    yl = _dot(lru_ref[...], wl_ref[0, 0])
    m = _sigmoid(ga_ref[...].astype(F32)) * ya + _sigmoid(gl_ref[...].astype(F32)) * yl
    o_ref[...] = m.astype(BF16)


def merge_branches(att, lru, proj, ga_off, D, w_branch, l):
    M, K = att.shape
    tm = _tile(M, 1024)
    tn = 1024
    while ga_off % tn or D % tn:
        tn //= 2
    ga_b, gl_b = ga_off // tn, (ga_off + D) // tn
    return pl.pallas_call(
        _merge_kernel,
        grid=(M // tm, D // tn),
        in_specs=[
            pl.BlockSpec((tm, K), lambda i, j: (i, 0)),
            pl.BlockSpec((tm, K), lambda i, j: (i, 0)),
            pl.BlockSpec((tm, tn), lambda i, j: (i, ga_b + j)),
            pl.BlockSpec((tm, tn), lambda i, j: (i, gl_b + j)),
            pl.BlockSpec((1, 1, K, tn), lambda i, j: (l, 0, 0, j)),
            pl.BlockSpec((1, 1, K, tn), lambda i, j: (l, 1, 0, j)),
        ],
        out_specs=pl.BlockSpec((tm, tn), lambda i, j: (i, j)),
        out_shape=jax.ShapeDtypeStruct((M, D), BF16),
        compiler_params=_params("arbitrary", "arbitrary"),
        name="merge",
    )(att, lru, proj, proj, w_branch, w_branch)


def _postln_kernel(*refs, gate_row, sh_row, sc_row, alpha, router, emit_h):
    x_ref, y_ref, mod_ref, modn_ref, lng_ref, lnb_ref = refs[:6]
    rest = list(refs[6:])
    wr_ref = rest.pop(0) if router else None
    xo_ref = rest.pop(0)
    ho_ref = rest.pop(0) if emit_h else None
    lg_ref = rest.pop(0) if router else None
    z = alpha * x_ref[0] + mod_ref[0, gate_row:gate_row + 1, :] * y_ref[0].astype(F32)
    mu = jnp.mean(z, axis=-1, keepdims=True)
    zc = z - mu
    var = jnp.mean(zc * zc, axis=-1, keepdims=True)
    xn = zc * lax.rsqrt(var + NORM_EPS) * lng_ref[...] + lnb_ref[...]
    xo_ref[0] = xn
    if emit_h:
        h = xn * (1.0 + modn_ref[0, sc_row:sc_row + 1, :]) + modn_ref[0, sh_row:sh_row + 1, :]
        ho_ref[0] = h.astype(BF16)
        if router:
            lg_ref[0] = lax.dot_general(wr_ref[...], h, (((1,), (1,)), ((), ())),
                                        preferred_element_type=F32, precision=HIGHEST)


def post_ln(x, y, mod, modn, ln_g, ln_b, alpha, gate_row, sh_row, sc_row, w_router_t=None, emit_h=True):
    B, T, D = x.shape
    tr = _tile(T, 256)
    router = w_router_t is not None
    mspec = lambda m: pl.BlockSpec((1, 6, D), (lambda b, i: (b, 0, 0)) if m.shape[0] > 1 else (lambda b, i: (0, 0, 0)))
    row = pl.BlockSpec((1, tr, D), lambda b, i: (b, i, 0))
    vec = pl.BlockSpec((1, D), lambda b, i: (0, 0))
    in_specs = [row, row, mspec(mod), mspec(modn), vec, vec]
    args = [x, y, mod, modn, ln_g.reshape(1, D), ln_b.reshape(1, D)]
    out_specs = [row]
    out_shape = [jax.ShapeDtypeStruct((B, T, D), F32)]
    if router:
        E = w_router_t.shape[0]
        in_specs.append(pl.BlockSpec((E, D), lambda b, i: (0, 0)))
        args.append(w_router_t)
    if emit_h:
        out_specs.append(row)
        out_shape.append(jax.ShapeDtypeStruct((B, T, D), BF16))
    if router:
        out_specs.append(pl.BlockSpec((1, E, tr), lambda b, i: (b, 0, i)))
        out_shape.append(jax.ShapeDtypeStruct((B, E, T), F32))
    kern = functools.partial(_postln_kernel, gate_row=gate_row, sh_row=sh_row, sc_row=sc_row, alpha=alpha,
                             router=router, emit_h=emit_h)
    return pl.pallas_call(
        kern,
        grid=(B, T // tr),
        in_specs=in_specs,
        out_specs=out_specs,
        out_shape=out_shape,
        compiler_params=_params("arbitrary", "arbitrary"),
        name="post_ln",
    )(*args)


def _topk_kernel(lg_ref, pos_ref, gate_ref, cnt_ref, *, T, cap, tw):
    lg = lg_ref[0]
    E = lg.shape[0]
    ex = jnp.exp(lg - jnp.max(lg, axis=0, keepdims=True))
    aff = ex / jnp.sum(ex, axis=0, keepdims=True)
    bits = lax.bitcast_convert_type(aff, jnp.int32)

    def count(mask):
        return jnp.sum(jnp.where(mask, 1.0, 0.0), axis=1, keepdims=True)

    thr = jnp.zeros((E, 1), jnp.int32)
    for bit in range(30, -1, -1):
        cand = thr | (1 << bit)
        thr = jnp.where(count(bits >= cand) >= cap, cand, thr)

    t_row = lax.broadcasted_iota(jnp.int32, (T, V7X_LANES), 0)
    w_col = lax.broadcasted_iota(jnp.int32, (T, V7X_LANES), 1)
    before_window = jnp.where(t_row < w_col * tw, 1.0, 0.0).astype(BF16)
    r_i = lax.broadcasted_iota(jnp.int32, (tw, tw), 0)
    c_i = lax.broadcasted_iota(jnp.int32, (tw, tw), 1)
    strict_upper = jnp.where(r_i < c_i, 1.0, 0.0).astype(BF16)

    def prefix(mask):
        mb = jnp.where(mask, 1.0, 0.0).astype(BF16)
        starts = _dot(mb, before_window)
        parts = [_dot(mb[:, w * tw:(w + 1) * tw], strict_upper) + starts[:, w:w + 1] for w in range(T // tw)]
        return jnp.concatenate(parts, axis=1) if len(parts) > 1 else parts[0], starts

    gt = bits > thr
    eq = bits == thr
    need = cap - count(gt)
    tie_rank, _ = prefix(eq)
    sel = gt | (eq & (tie_rank < need))
    slot, starts = prefix(sel)
    pos_ref[0] = jnp.where(sel, slot.astype(jnp.int32), -1)
    gate_ref[0] = jnp.where(sel, aff, 0.0)
    cnt_ref[0] = starts.astype(jnp.int32)


def expert_topk(logits, cap):
    B, E, T = logits.shape
    tw = min(TOKEN_WINDOW, T)
    kern = functools.partial(_topk_kernel, T=T, cap=cap, tw=tw)
    spec = pl.BlockSpec((1, E, T), lambda b: (b, 0, 0))
    return pl.pallas_call(
        kern,
        grid=(B,),
        in_specs=[spec],
        out_specs=[spec, spec, pl.BlockSpec((1, E, V7X_LANES), lambda b: (b, 0, 0))],
        out_shape=[jax.ShapeDtypeStruct((B, E, T), jnp.int32), jax.ShapeDtypeStruct((B, E, T), F32),
                   jax.ShapeDtypeStruct((B, E, V7X_LANES), jnp.int32)],
        compiler_params=_params("arbitrary"),
        name="expert_topk",
    )(logits)


def _slot_span(cnt_ref, idx, C, sp):
    lo = cnt_ref[idx]
    hi = cnt_ref[idx + 1]
    start = jnp.minimum((lo // BF16_ROWS) * BF16_ROWS, C - sp)
    n_extra = jnp.maximum(hi - start - 1, 0) // sp
    return pl.multiple_of(start, BF16_ROWS), n_extra


def _gather_kernel(cnt_ref, pos_ref, h_ref, xs_ref, *, C, sp, nw, E):
    b = pl.program_id(0)
    w = pl.program_id(2)

    @pl.when(w == 0)
    def _():
        xs_ref[...] = jnp.zeros(xs_ref.shape, BF16)

    hwin = h_ref[0]
    tw = hwin.shape[0]
    slot_i = lax.broadcasted_iota(jnp.int32, (sp, tw), 0)
    spans = [_slot_span(cnt_ref, (b * E + e) * (nw + 1) + w, C, sp) for e in range(E)]
    onehots = [jnp.where(pos_ref[0, e:e + 1, :] == slot_i + spans[e][0], 1.0, 0.0).astype(BF16)
               for e in range(E)]
    rows = _dot(jnp.concatenate(onehots, axis=0), hwin)
    for e in range(E):
        start, n_extra = spans[e]
        xs_ref[0, e, pl.ds(start, sp), :] += rows[e * sp:(e + 1) * sp].astype(BF16)

        def extra(i, carry):
            j0 = start + sp * (i + 1)
            j0c = pl.multiple_of(jnp.minimum(j0, C - sp), BF16_ROWS)
            slot = slot_i + j0c
            hit = (pos_ref[0, e:e + 1, :] == slot) & (slot >= j0)
            xs_ref[0, e, pl.ds(j0c, sp), :] += _dot(jnp.where(hit, 1.0, 0.0).astype(BF16), hwin).astype(BF16)
            return carry

        lax.fori_loop(0, n_extra, extra, 0)


def gather_tokens(h, pos, cnt_flat, C):
    B, T, D = h.shape
    E = pos.shape[1]
    tw = min(TOKEN_WINDOW, T)
    nw = T // tw
    sp = min(SLOT_SPAN, C)
    dc = _tile(D, 1024)
    kern = functools.partial(_gather_kernel, C=C, sp=sp, nw=nw, E=E)
    grid_spec = pltpu.PrefetchScalarGridSpec(
        num_scalar_prefetch=1,
        grid=(B, D // dc, nw),
        in_specs=[
            pl.BlockSpec((1, E, tw), lambda b, d, w, cnt: (b, 0, w)),
            pl.BlockSpec((1, tw, dc), lambda b, d, w, cnt: (b, w, d)),
        ],
        out_specs=pl.BlockSpec((1, E, C, dc), lambda b, d, w, cnt: (b, 0, 0, d)),
    )
    return pl.pallas_call(
        kern,
        grid_spec=grid_spec,
        out_shape=jax.ShapeDtypeStruct((B, E, C, D), BF16),
        compiler_params=_params("arbitrary", "arbitrary", "arbitrary"),
        name="moe_gather",
    )(cnt_flat, pos, h)


def _ffn_kernel(xs_ref, wgu_ref, wd_ref, y_ref, *, F):
    gu = _dot(xs_ref[0, 0], wgu_ref[0, 0])
    g = gu[:, :F]
    u = gu[:, F:]
    act = (g * _sigmoid(g) * u).astype(BF16)
    y_ref[0, 0] = _dot(act, wd_ref[0, 0]).astype(BF16)


def expert_ffn(xs, w_gate_up, w_down, l):
    B, E, C, D = xs.shape
    F = w_down.shape[2]
    kern = functools.partial(_ffn_kernel, F=F)
    return pl.pallas_call(
        kern,
        grid=(E, B),
        in_specs=[
            pl.BlockSpec((1, 1, C, D), lambda e, b: (b, e, 0, 0)),
            pl.BlockSpec((1, 1, D, 2 * F), lambda e, b: (l, e, 0, 0)),
            pl.BlockSpec((1, 1, F, D), lambda e, b: (l, e, 0, 0)),
        ],
        out_specs=pl.BlockSpec((1, 1, C, D), lambda e, b: (b, e, 0, 0)),
        out_shape=jax.ShapeDtypeStruct((B, E, C, D), BF16),
        compiler_params=_params("arbitrary", "arbitrary"),
        name="expert_ffn",
    )(xs, w_gate_up, w_down)


def _scatter_kernel(cnt_ref, post_ref, gatet_ref, y_ref, o_ref, acc_ref, *, C, sp, nw, E):
    b = pl.program_id(0)
    w = pl.program_id(2)
    tw = post_ref.shape[1]
    lane = lax.broadcasted_iota(jnp.int32, (tw, 2 * sp), 1)
    lane_slot = lane & (sp - 1)
    first_half = lane < sp

    def gated_onehot(e, j0c, j0):
        pcol = post_ref[0, :, e:e + 1]
        g = gatet_ref[0, :, e:e + 1]
        g_hi = g.astype(BF16).astype(F32)
        slot = lane_slot + j0c
        hit = pcol == slot
        if j0 is not None:
            hit = hit & (slot >= j0)
        return jnp.where(hit, jnp.where(first_half, g_hi, g - g_hi), 0.0).astype(BF16)

    def contribution(experts, starts, j0):
        lhs = [gated_onehot(e, s, j0) for e, s in zip(experts, starts)]
        rhs = []
        for e, s in zip(experts, starts):
            ysp = y_ref[0, e, pl.ds(s, sp), :]
            rhs += [ysp, ysp]
        return _dot(lhs[0] if len(lhs) == 1 else jnp.concatenate(lhs, axis=1), jnp.concatenate(rhs, axis=0))

    spans = [_slot_span(cnt_ref, (b * E + e) * (nw + 1) + w, C, sp) for e in range(E)]
    total = None
    for e in range(0, E, 2):
        r = contribution((e, e + 1), (spans[e][0], spans[e + 1][0]), None)
        total = r if total is None else total + r
    acc_ref[...] = total
    for e in range(E):
        start, n_extra = spans[e]

        def extra(i, carry):
            j0 = start + sp * (i + 1)
            j0c = pl.multiple_of(jnp.minimum(j0, C - sp), BF16_ROWS)
            acc_ref[...] += contribution((e,), (j0c,), j0)
            return carry

        lax.fori_loop(0, n_extra, extra, 0)
    o_ref[0] = acc_ref[...].astype(o_ref.dtype)


def scatter_tokens(y, pos_t, gate_t, cnt_flat):
    B, E, C, D = y.shape
    T = pos_t.shape[1]
    tw = min(TOKEN_WINDOW, T)
    nw = T // tw
    sp = min(SLOT_SPAN, C)
    dc = _tile(D, 1024)
    kern = functools.partial(_scatter_kernel, C=C, sp=sp, nw=nw, E=E)
    grid_spec = pltpu.PrefetchScalarGridSpec(
        num_scalar_prefetch=1,
        grid=(B, D // dc, nw),
        in_specs=[
            pl.BlockSpec((1, tw, E), lambda b, d, w, cnt: (b, w, 0)),
            pl.BlockSpec((1, tw, E), lambda b, d, w, cnt: (b, w, 0)),
            pl.BlockSpec((1, E, C, dc), lambda b, d, w, cnt: (b, 0, 0, d)),
        ],
        out_specs=pl.BlockSpec((1, tw, dc), lambda b, d, w, cnt: (b, w, d)),
        scratch_shapes=[pltpu.VMEM((tw, dc), F32)],
    )
    return pl.pallas_call(
        kern,
        grid_spec=grid_spec,
        out_shape=jax.ShapeDtypeStruct((B, T, D), BF16),
        compiler_params=_params("arbitrary", "arbitrary", "arbitrary"),
        name="moe_scatter",
    )(cnt_flat, pos_t, gate_t, y)


def expert_choice_ffn(h, logits, w_gate_up, w_down, l):
    B, T, D = h.shape
    E = logits.shape[1]
    cap = CAP_FACTOR * T // E
    nw = T // min(TOKEN_WINDOW, T)
    pos, gate, cnt = expert_topk(logits, cap)
    cnt_flat = cnt[:, :, :nw + 1].reshape(-1)
    xs = gather_tokens(h, pos, cnt_flat, cap)
    y = expert_ffn(xs, w_gate_up, w_down, l)
    return scatter_tokens(y, jnp.swapaxes(pos, 1, 2), jnp.swapaxes(gate, 1, 2), cnt_flat)


def _rope_tables(T):
    rows = T // GRID_W
    row = jnp.repeat(jnp.arange(rows, dtype=F32), GRID_W)
    col = jnp.tile(jnp.arange(GRID_W, dtype=F32), rows)
    n_freq = HEAD_DIM // 4
    inv = ROPE_THETA ** (-jnp.arange(n_freq, dtype=F32) / n_freq)
    ang = jnp.concatenate([row[:, None] * inv, col[:, None] * inv], axis=-1)
    cos2 = jnp.repeat(jnp.cos(ang), 2, axis=-1)
    sin = jnp.sin(ang)
    sin2 = jnp.stack([-sin, sin], axis=-1).reshape(T, HEAD_DIM)
    return cos2, sin2


def kernel(x, c, ctx, c_ctx, w_ada_dn, w_ada_up, b_ada, w_in, q_norm_g, k_norm_g, conv_w, conv_b, w_rg, b_rg, w_ig, b_ig, lru_lambda, w_branch, w_out, ln_g, ln_b, w_router, w_gate_up, w_down):
    B, T, D = x.shape
    Tc = ctx.shape[1]
    depth = w_in.shape[0]
    in_w = w_in.shape[2]
    lru_w = conv_w.shape[2]
    kvw = (in_w - 2 * lru_w - 2 * D) // (GQA_GROUP + 2)
    qw = GQA_GROUP * kvw
    q_off = 2 * lru_w
    ga_off = q_off + qw + 2 * kvw
    alpha = (2.0 * depth) ** 0.25

    n_rows = -(-(B + 1) // V7X_SUBLANES) * V7X_SUBLANES
    cv = jnp.zeros((n_rows, D), F32).at[:B].set(c).at[B].set(c_ctx)
    mods = ada_modulation(cv, w_ada_dn, w_ada_up, b_ada).reshape(depth, n_rows, 6, D)
    mod_l = [mods[l, :B] for l in range(depth)]
    mod_c = [mods[l, B:B + 1] for l in range(depth)]

    cos2, sin2 = _rope_tables(T)
    w_in_b = w_in.astype(BF16)
    w_branch_b = w_branch.astype(BF16)
    w_out_b = w_out.astype(BF16)
    w_rg_b = w_rg.astype(BF16)
    w_ig_b = w_ig.astype(BF16)
    w_gu_b = w_gate_up.astype(BF16)
    w_dn_b = w_down.astype(BF16)
    w_router_t = jnp.swapaxes(w_router, 1, 2)

    xl, xc = x, ctx
    hl = modulate_rows(xl, mod_l[0])
    hc = modulate_rows(xc, mod_c[0])
    for l in range(depth):
        need_ctx = l < depth - 1
        last = l == depth - 1
        proj_l = matmul(hl.reshape(B * T, D), w_in_b, l).reshape(B, T, in_w)
        proj_c = matmul(hc.reshape(B * Tc, D), w_in_b, l).reshape(B, Tc, in_w)
        q_l, k_l, v_l = norm_rope(proj_l, q_off, qw, kvw, q_norm_g[l], k_norm_g[l], cos2, sin2)
        q_c, k_c, v_c = norm_rope(proj_c, q_off, qw, kvw, q_norm_g[l], k_norm_g[l], None, None)
        k_all = jnp.concatenate([k_c, k_l], axis=1)
        v_all = jnp.concatenate([v_c, v_l], axis=1)
        att_l = attention(q_l, k_all, v_all)
        lru_l, lru_c = lru_branch(proj_l, proj_c, lru_w, l, conv_w, conv_b, w_rg_b, b_rg, w_ig_b, b_ig, lru_lambda)

        def tail(xs, hs_proj, att, lru, mod, mod_next, Ts):
            m = merge_branches(att.reshape(B * Ts, qw), lru.reshape(B * Ts, lru_w), hs_proj.reshape(B * Ts, in_w),
                               ga_off, D, w_branch_b, l)
            mix = matmul(m, w_out_b, l).reshape(B, Ts, D)
            x1, h2, logits = post_ln(xs, mix, mod, mod, ln_g[l, 0], ln_b[l, 0], alpha, 2, 3, 4, w_router_t[l])
            moe = expert_choice_ffn(h2, logits, w_gu_b, w_dn_b, l)
            if last:
                (x2,) = post_ln(x1, moe, mod, mod, ln_g[l, 1], ln_b[l, 1], alpha, 5, 0, 1, emit_h=False)
                return x2, None
            return post_ln(x1, moe, mod, mod_next, ln_g[l, 1], ln_b[l, 1], alpha, 5, 0, 1)

        xl, hl = tail(xl, proj_l, att_l, lru_l, mod_l[l], None if last else mod_l[l + 1], T)
        if need_ctx:
            att_c = attention(q_c, k_c, v_c)
            xc, hc = tail(xc, proj_c, att_c, lru_c, mod_c[l], mod_c[l + 1], Tc)
    return xl
```

```python
import functools
import math

import jax
import jax.numpy as jnp
from jax import lax
from jax.experimental import pallas as pl
from jax.experimental.pallas import tpu as pltpu

HEAD_DIM = 128
GQA_GROUP = 4
GRID_W = 64
ROPE_THETA = 10000.0
LRU_BLOCK = 128
LRU_C = 8.0
CONV_W = 4
CONV_LEFT = 2
N_EXPERTS = 16
CAP_FACTOR = 2
NORM_EPS = 1e-6
LOG2_E = 1.4426950408889634
GELU_K0 = 0.7978845608028654
GELU_K1 = 0.044715
TINY_F32 = 1e-30
LRU_SCAN_UNROLL = 4

V7X_LANES = 128
V7X_SUBLANES = 8
V7X_VMEM_LIMIT_BYTES = 56 * 1024 * 1024

ATTN_TQ = 512
ATTN_TK = 1024
ATTN_SUB = 128
TOKEN_WINDOW = 256
SLOT_SPAN = 64
BF16_ROWS = 16

F32 = jnp.float32
BF16 = jnp.bfloat16
HIGHEST = lax.Precision.HIGHEST


def _tile(n, pref):
    t = min(pref, n)
    while n % t:
        t //= 2
    return t


def _params(*sem):
    return pltpu.CompilerParams(dimension_semantics=sem, vmem_limit_bytes=V7X_VMEM_LIMIT_BYTES)


def _sigmoid(x):
    return 1.0 / (1.0 + jnp.exp(-x))


def _dot(a, b):
    return jnp.dot(a, b, preferred_element_type=F32)


def _ada_kernel(c_ref, wdn_ref, wup_ref, b_ref, o_ref):
    c = c_ref[...]
    s = c * _sigmoid(c)
    t = jnp.dot(s, wdn_ref[0], preferred_element_type=F32, precision=HIGHEST)
    o_ref[0] = jnp.dot(t, wup_ref[0], preferred_element_type=F32, precision=HIGHEST) + b_ref[0]


def ada_modulation(cv, w_dn, w_up, b_up):
    L, D, R = w_dn.shape
    N = w_up.shape[2]
    rows = cv.shape[0]
    tn = _tile(N, 4096)
    return pl.pallas_call(
        _ada_kernel,
        grid=(L, N // tn),
        in_specs=[
            pl.BlockSpec((rows, D), lambda l, j: (0, 0)),
            pl.BlockSpec((1, D, R), lambda l, j: (l, 0, 0)),
            pl.BlockSpec((1, R, tn), lambda l, j: (l, 0, j)),
            pl.BlockSpec((1, 1, tn), lambda l, j: (l, 0, j)),
        ],
        out_specs=pl.BlockSpec((1, rows, tn), lambda l, j: (l, 0, j)),
        out_shape=jax.ShapeDtypeStruct((L, rows, N), F32),
        compiler_params=_params("arbitrary", "arbitrary"),
        name="ada_modulation",
    )(cv, w_dn, w_up, b_up.reshape(L, 1, N))


def _modulate_kernel(x_ref, mod_ref, h_ref):
    sh = mod_ref[0, 0:1, :]
    sc = mod_ref[0, 1:2, :]
    h_ref[0] = (x_ref[0] * (1.0 + sc) + sh).astype(BF16)


def modulate_rows(x, mod):
    B, T, D = x.shape
    tr = _tile(T, 512)
    per_batch = mod.shape[0] > 1
    return pl.pallas_call(
        _modulate_kernel,
        grid=(B, T // tr),
        in_specs=[
            pl.BlockSpec((1, tr, D), lambda b, i: (b, i, 0)),
            pl.BlockSpec((1, 6, D), (lambda b, i: (b, 0, 0)) if per_batch else (lambda b, i: (0, 0, 0))),
        ],
        out_specs=pl.BlockSpec((1, tr, D), lambda b, i: (b, i, 0)),
        out_shape=jax.ShapeDtypeStruct((B, T, D), BF16),
        compiler_params=_params("arbitrary", "arbitrary"),
        name="modulate",
    )(x, mod)


def _matmul_kernel(a_ref, w_ref, o_ref):
    o_ref[...] = _dot(a_ref[...], w_ref[0]).astype(o_ref.dtype)


def matmul(a, w, l, out_dtype=BF16):
    M, K = a.shape
    N = w.shape[2]
    tm, tn = _tile(M, 1024), _tile(N, 1024)
    return pl.pallas_call(
        _matmul_kernel,
        grid=(M // tm, N // tn),
        in_specs=[
            pl.BlockSpec((tm, K), lambda i, j: (i, 0)),
            pl.BlockSpec((1, K, tn), lambda i, j: (l, 0, j)),
        ],
        out_specs=pl.BlockSpec((tm, tn), lambda i, j: (i, j)),
        out_shape=jax.ShapeDtypeStruct((M, N), out_dtype),
        compiler_params=_params("arbitrary", "arbitrary"),
        name="matmul",
    )(a, w)


def _normrope_kernel(*refs, nq, nkv, rope, scale):
    if rope:
        q_ref, kv_ref, qg_ref, kg_ref, cos_ref, sin_ref, qo_ref, ko_ref, vo_ref = refs
        cs = cos_ref[...]
        sn = sin_ref[...]
        lane = lax.broadcasted_iota(jnp.int32, cs.shape, 1)
        even = (lane & 1) == 0
    else:
        q_ref, kv_ref, qg_ref, kg_ref, qo_ref, ko_ref, vo_ref = refs

    def head(xh, g):
        xf = xh.astype(F32)
        y = xf * lax.rsqrt(jnp.mean(xf * xf, axis=-1, keepdims=True) + NORM_EPS) * g
        if rope:
            partner = jnp.where(even, pltpu.roll(y, HEAD_DIM - 1, 1), pltpu.roll(y, 1, 1))
            y = y * cs + partner * sn
        return y

    qg = qg_ref[...]
    kg = kg_ref[...]
    for h in range(nq):
        sl = slice(h * HEAD_DIM, (h + 1) * HEAD_DIM)
        qo_ref[0, :, sl] = (head(q_ref[0, :, sl], qg) * scale).astype(BF16)
    for h in range(nkv):
        sl = slice(h * HEAD_DIM, (h + 1) * HEAD_DIM)
        ko_ref[0, :, sl] = head(kv_ref[0, :, sl], kg).astype(BF16)
    vo_ref[0] = kv_ref[0, :, nkv * HEAD_DIM:]


def norm_rope(proj, q_off, qw, kvw, qg, kg, cos2, sin2):
    B, T, _ = proj.shape
    rope = cos2 is not None
    tr = _tile(T, 512)
    assert q_off % qw == 0 and (q_off + qw) % (2 * kvw) == 0
    qb, kvb = q_off // qw, (q_off + qw) // (2 * kvw)
    in_specs = [
        pl.BlockSpec((1, tr, qw), lambda b, i: (b, i, qb)),
        pl.BlockSpec((1, tr, 2 * kvw), lambda b, i: (b, i, kvb)),
        pl.BlockSpec((1, HEAD_DIM), lambda b, i: (0, 0)),
        pl.BlockSpec((1, HEAD_DIM), lambda b, i: (0, 0)),
    ]
    args = [proj, proj, qg.reshape(1, HEAD_DIM), kg.reshape(1, HEAD_DIM)]
    if rope:
        in_specs += [pl.BlockSpec((tr, HEAD_DIM), lambda b, i: (i, 0))] * 2
        args += [cos2, sin2]
    kern = functools.partial(_normrope_kernel, nq=qw // HEAD_DIM, nkv=kvw // HEAD_DIM, rope=rope,
                             scale=HEAD_DIM ** -0.5)
    return pl.pallas_call(
        kern,
        grid=(B, T // tr),
        in_specs=in_specs,
        out_specs=[
            pl.BlockSpec((1, tr, qw), lambda b, i: (b, i, 0)),
            pl.BlockSpec((1, tr, kvw), lambda b, i: (b, i, 0)),
            pl.BlockSpec((1, tr, kvw), lambda b, i: (b, i, 0)),
        ],
        out_shape=[
            jax.ShapeDtypeStruct((B, T, qw), BF16),
            jax.ShapeDtypeStruct((B, T, kvw), BF16),
            jax.ShapeDtypeStruct((B, T, kvw), BF16),
        ],
        compiler_params=_params("arbitrary", "arbitrary"),
        name="norm_rope",
    )(*args)


def _attn_kernel(q_ref, k_ref, v_ref, o_ref, vx_ref, m_ref, acc_ref, *, tq, first, tk, n_chunks):
    @pl.when(pl.program_id(2) == 0)
    def _():
        vx_ref[:, 0:HEAD_DIM] = v_ref[0]
        vx_ref[:, HEAD_DIM:] = jnp.ones((vx_ref.shape[0], HEAD_DIM), BF16)

    q = jnp.concatenate([q_ref[0, :, g * HEAD_DIM:(g + 1) * HEAD_DIM] for g in range(GQA_GROUP)], axis=0)
    m_ref[...] = jnp.full(m_ref.shape, -jnp.inf, F32)
    acc_ref[...] = jnp.zeros(acc_ref.shape, F32)

    def chunk(off, size):
        k = k_ref[0, pl.ds(off, size), :]
        vx = vx_ref[pl.ds(off, size), :]
        for r in range(0, GQA_GROUP * tq, ATTN_SUB):
            rows = slice(r, r + ATTN_SUB)
            s = lax.dot_general(q[rows], k, (((1,), (1,)), ((), ())), preferred_element_type=F32)
            m_prev = m_ref[rows, :]
            m_next = jnp.maximum(m_prev, jnp.max(s, axis=-1, keepdims=True))
            alpha = jnp.exp(m_prev - m_next)
            p = jnp.exp(s - jnp.concatenate([m_next] * (size // HEAD_DIM), axis=1))
            acc_ref[rows, :] = jnp.concatenate([alpha, alpha], axis=1) * acc_ref[rows, :] + _dot(p.astype(BF16), vx)
            m_ref[rows, :] = m_next

    chunk(0, first)
    for i in range(n_chunks):
        chunk(first + i * tk, tk)
    acc = acc_ref[...]
    o = acc[:, :HEAD_DIM] * (1.0 / acc[:, HEAD_DIM:])
    for g in range(GQA_GROUP):
        o_ref[0, :, g * HEAD_DIM:(g + 1) * HEAD_DIM] = o[g * tq:(g + 1) * tq].astype(BF16)


def attention(q, k, v):
    B, Tq, QW = q.shape
    Tk, KVW = k.shape[1], k.shape[2]
    nkv = KVW // HEAD_DIM
    tq = _tile(Tq, ATTN_TQ)
    tk = min(ATTN_TK, Tk)
    n_chunks = (Tk - 1) // tk
    first = Tk - n_chunks * tk
    assert first % HEAD_DIM == 0 and tk % HEAD_DIM == 0
    gw = GQA_GROUP * HEAD_DIM
    kern = functools.partial(_attn_kernel, tq=tq, first=first, tk=tk, n_chunks=n_chunks)
    return pl.pallas_call(
        kern,
        grid=(B, nkv, Tq // tq),
        in_specs=[
            pl.BlockSpec((1, tq, gw), lambda b, h, i: (b, i, h)),
            pl.BlockSpec((1, Tk, HEAD_DIM), lambda b, h, i: (b, 0, h)),
            pl.BlockSpec((1, Tk, HEAD_DIM), lambda b, h, i: (b, 0, h)),
        ],
        out_specs=pl.BlockSpec((1, tq, gw), lambda b, h, i: (b, i, h)),
        out_shape=jax.ShapeDtypeStruct((B, Tq, QW), BF16),
        scratch_shapes=[
            pltpu.VMEM((Tk, 2 * HEAD_DIM), BF16),
            pltpu.VMEM((GQA_GROUP * tq, HEAD_DIM), F32),
            pltpu.VMEM((GQA_GROUP * tq, 2 * HEAD_DIM), F32),
        ],
        compiler_params=_params("arbitrary", "arbitrary", "arbitrary"),
        name="attention",
    )(q, k, v)


def _gelu_tanh(x):
    half = 0.5 * x
    return half + half * jnp.tanh(x * (GELU_K0 + (GELU_K0 * GELU_K1) * (x * x)))


def _lru_kernel(xl_ref, gl_ref, xc_ref, gc_ref, cw_ref, cb_ref, wr_ref, br_ref, wi_ref, bi_ref, lam_ref,
                yl_ref, yc_ref, pad_ref, al_ref, bl_ref, hl_ref, ac_ref, bc_ref, hc_ref, *, Ll, Lc):
    S = V7X_SUBLANES
    cw = cw_ref[0]
    cb = cb_ref[0]

    def coeffs(x_ref, L, a_ref, b_ref):
        pad_ref[0:S, :] = jnp.zeros((S, LRU_BLOCK), F32)
        pad_ref[S:S + L, :] = x_ref[0].astype(F32)
        pad_ref[S + L:2 * S + L, :] = jnp.zeros((S, LRU_BLOCK), F32)
        u = cb
        for j in range(CONV_W):
            start = S - CONV_LEFT + j
            u = u + pad_ref[start:start + L, :] * cw[j:j + 1, :]
        ub = u.astype(BF16)
        half_u = 0.5 * u
        for d in range(2):
            tr = jnp.tanh(0.5 * (_dot(ub, wr_ref[0, d, 0]) + br_ref[0, d:d + 1, :]))
            ti = jnp.tanh(0.5 * (_dot(ub, wi_ref[0, d, 0]) + bi_ref[0, d:d + 1, :]))
            nl = -lam_ref[0, d:d + 1, :]
            softplus = jnp.maximum(nl, 0.0) + jnp.log(1.0 + jnp.exp(-jnp.abs(nl)))
            k = (-0.5 * LRU_C * LOG2_E) * softplus
            a = jnp.exp2(k + k * tr)
            a_ref[d] = a
            gap = 1.0 - a * a
            b_ref[d] = gap * lax.rsqrt(jnp.maximum(gap, TINY_F32)) * (half_u + half_u * ti)

    row = lax.broadcasted_iota(jnp.int32, (S, LRU_BLOCK), 0)

    def tile_scan(a, b, carry, reverse):
        for d in (1, 2, 4):
            if reverse:
                keep = row < S - d
                shift = S - d
            else:
                keep = row >= d
                shift = d
            b = b + a * jnp.where(keep, pltpu.roll(b, shift, 0), 0.0)
            a = a * jnp.where(keep, pltpu.roll(a, shift, 0), 1.0)
        h = b + a * carry
        last = h[0:1, :] if reverse else h[S - 1:S, :]
        return h, jnp.broadcast_to(last, (S, LRU_BLOCK))

    def scan(L, a_ref, b_ref, h_ref, carry_f, carry_r):
        n = L // S

        def body(s, carry):
            cf, cr = carry
            for j in range(LRU_SCAN_UNROLL):
                rf = pl.ds(pl.multiple_of((s * LRU_SCAN_UNROLL + j) * S, S), S)
                hf, cf = tile_scan(a_ref[0, rf, :], b_ref[0, rf, :], cf, False)
                h_ref[0, rf, :] = hf
                rr = pl.ds(pl.multiple_of((n - 1 - s * LRU_SCAN_UNROLL - j) * S, S), S)
                hr, cr = tile_scan(a_ref[1, rr, :], b_ref[1, rr, :], cr, True)
                h_ref[1, rr, :] = hr
            return cf, cr

        return lax.fori_loop(0, n // LRU_SCAN_UNROLL, body, (carry_f, carry_r))

    def finish(h_ref, g_ref, y_ref):
        y_ref[0] = ((h_ref[0] + h_ref[1]) * _gelu_tanh(g_ref[0].astype(F32))).astype(BF16)

    coeffs(xc_ref, Lc, ac_ref, bc_ref)
    coeffs(xl_ref, Ll, al_ref, bl_ref)
    zero = jnp.zeros((S, LRU_BLOCK), F32)
    end_f, end_r = scan(Lc, ac_ref, bc_ref, hc_ref, zero, zero)
    scan(Ll, al_ref, bl_ref, hl_ref, end_f, end_r)
    finish(hc_ref, gc_ref, yc_ref)
    finish(hl_ref, gl_ref, yl_ref)


def lru_branch(proj_l, proj_c, lru_w, l, conv_w, conv_b, w_rg, b_rg, w_ig, b_ig, lam):
    B, Ll, _ = proj_l.shape
    Lc = proj_c.shape[1]
    nb = lru_w // LRU_BLOCK
    S = V7X_SUBLANES
    assert Ll % (S * LRU_SCAN_UNROLL) == 0 and Lc % (S * LRU_SCAN_UNROLL) == 0
    blk = lambda L, off: pl.BlockSpec((1, L, LRU_BLOCK), lambda b, h: (b, 0, h + off))
    chan = lambda rows: pl.BlockSpec((1, rows, LRU_BLOCK), lambda b, h: (l, 0, h))
    wspec = pl.BlockSpec((1, 2, 1, LRU_BLOCK, LRU_BLOCK), lambda b, h: (l, 0, h, 0, 0))
    kern = functools.partial(_lru_kernel, Ll=Ll, Lc=Lc)
    return pl.pallas_call(
        kern,
        grid=(B, nb),
        in_specs=[blk(Ll, 0), blk(Ll, nb), blk(Lc, 0), blk(Lc, nb), chan(CONV_W), chan(1),
                  wspec, chan(2), wspec, chan(2), chan(2)],
        out_specs=[pl.BlockSpec((1, Ll, LRU_BLOCK), lambda b, h: (b, 0, h)),
                   pl.BlockSpec((1, Lc, LRU_BLOCK), lambda b, h: (b, 0, h))],
        out_shape=[jax.ShapeDtypeStruct((B, Ll, lru_w), BF16), jax.ShapeDtypeStruct((B, Lc, lru_w), BF16)],
        scratch_shapes=[pltpu.VMEM((Ll + 2 * S, LRU_BLOCK), F32)]
        + [pltpu.VMEM((2, Ll, LRU_BLOCK), F32)] * 3 + [pltpu.VMEM((2, Lc, LRU_BLOCK), F32)] * 3,
        compiler_params=_params("arbitrary", "arbitrary"),
        name="rglru",
    )(proj_l, proj_l, proj_c, proj_c, conv_w, conv_b.reshape(conv_b.shape[0], 1, lru_w), w_rg, b_rg, w_ig, b_ig, lam)


def _merge_kernel(att_ref, lru_ref, ga_ref, gl_ref, wa_ref, wl_ref, o_ref):
    ya = _dot(att_ref[...], wa_ref[0, 0])
    yl = _dot(lru_ref[...], wl_ref[0, 0])
    m = _sigmoid(ga_ref[...].astype(F32)) * ya + _sigmoid(gl_ref[...].astype(F32)) * yl
    o_ref[...] = m.astype(BF16)


def merge_branches(att, lru, proj, ga_off, D, w_branch, l):
    M, K = att.shape
    tm = _tile(M, 1024)
    tn = 1024
    while ga_off % tn or D % tn:
        tn //= 2
    ga_b, gl_b = ga_off // tn, (ga_off + D) // tn
    return pl.pallas_call(
        _merge_kernel,
        grid=(M // tm, D // tn),
        in_specs=[
            pl.BlockSpec((tm, K), lambda i, j: (i, 0)),
            pl.BlockSpec((tm, K), lambda i, j: (i, 0)),
            pl.BlockSpec((tm, tn), lambda i, j: (i, ga_b + j)),
            pl.BlockSpec((tm, tn), lambda i, j: (i, gl_b + j)),
            pl.BlockSpec((1, 1, K, tn), lambda i, j: (l, 0, 0, j)),
            pl.BlockSpec((1, 1, K, tn), lambda i, j: (l, 1, 0, j)),
        ],
        out_specs=pl.BlockSpec((tm, tn), lambda i, j: (i, j)),
        out_shape=jax.ShapeDtypeStruct((M, D), BF16),
        compiler_params=_params("arbitrary", "arbitrary"),
        name="merge",
    )(att, lru, proj, proj, w_branch, w_branch)


def _postln_kernel(*refs, gate_row, sh_row, sc_row, alpha, router, emit_h):
    x_ref, y_ref, mod_ref, modn_ref, lng_ref, lnb_ref = refs[:6]
    rest = list(refs[6:])
    wr_ref = rest.pop(0) if router else None
    xo_ref = rest.pop(0)
    ho_ref = rest.pop(0) if emit_h else None
    lg_ref = rest.pop(0) if router else None
    z = alpha * x_ref[0] + mod_ref[0, gate_row:gate_row + 1, :] * y_ref[0].astype(F32)
    mu = jnp.mean(z, axis=-1, keepdims=True)
    zc = z - mu
    var = jnp.mean(zc * zc, axis=-1, keepdims=True)
    xn = zc * lax.rsqrt(var + NORM_EPS) * lng_ref[...] + lnb_ref[...]
    xo_ref[0] = xn
    if emit_h:
        h = xn * (1.0 + modn_ref[0, sc_row:sc_row + 1, :]) + modn_ref[0, sh_row:sh_row + 1, :]
        ho_ref[0] = h.astype(BF16)
        if router:
            lg_ref[0] = lax.dot_general(wr_ref[...], h, (((1,), (1,)), ((), ())),
                                        preferred_element_type=F32, precision=HIGHEST)


def post_ln(x, y, mod, modn, ln_g, ln_b, alpha, gate_row, sh_row, sc_row, w_router_t=None, emit_h=True):
    B, T, D = x.shape
    tr = _tile(T, 256)
    router = w_router_t is not None
    mspec = lambda m: pl.BlockSpec((1, 6, D), (lambda b, i: (b, 0, 0)) if m.shape[0] > 1 else (lambda b, i: (0, 0, 0)))
    row = pl.BlockSpec((1, tr, D), lambda b, i: (b, i, 0))
    vec = pl.BlockSpec((1, D), lambda b, i: (0, 0))
    in_specs = [row, row, mspec(mod), mspec(modn), vec, vec]
    args = [x, y, mod, modn, ln_g.reshape(1, D), ln_b.reshape(1, D)]
    out_specs = [row]
    out_shape = [jax.ShapeDtypeStruct((B, T, D), F32)]
    if router:
        E = w_router_t.shape[0]
        in_specs.append(pl.BlockSpec((E, D), lambda b, i: (0, 0)))
        args.append(w_router_t)
    if emit_h:
        out_specs.append(row)
        out_shape.append(jax.ShapeDtypeStruct((B, T, D), BF16))
    if router:
        out_specs.append(pl.BlockSpec((1, E, tr), lambda b, i: (b, 0, i)))
        out_shape.append(jax.ShapeDtypeStruct((B, E, T), F32))
    kern = functools.partial(_postln_kernel, gate_row=gate_row, sh_row=sh_row, sc_row=sc_row, alpha=alpha,
                             router=router, emit_h=emit_h)
    return pl.pallas_call(
        kern,
        grid=(B, T // tr),
        in_specs=in_specs,
        out_specs=out_specs,
        out_shape=out_shape,
        compiler_params=_params("arbitrary", "arbitrary"),
        name="post_ln",
    )(*args)


def _topk_kernel(lg_ref, pos_ref, gate_ref, cnt_ref, *, T, cap, tw):
    lg = lg_ref[0]
    E = lg.shape[0]
    ex = jnp.exp(lg - jnp.max(lg, axis=0, keepdims=True))
    aff = ex / jnp.sum(ex, axis=0, keepdims=True)
    bits = lax.bitcast_convert_type(aff, jnp.int32)

    def count(mask):
        return jnp.sum(jnp.where(mask, 1.0, 0.0), axis=1, keepdims=True)

    thr = jnp.zeros((E, 1), jnp.int32)
    for bit in range(30, -1, -1):
        cand = thr | (1 << bit)
        thr = jnp.where(count(bits >= cand) >= cap, cand, thr)

    t_row = lax.broadcasted_iota(jnp.int32, (T, V7X_LANES), 0)
    w_col = lax.broadcasted_iota(jnp.int32, (T, V7X_LANES), 1)
    before_window = jnp.where(t_row < w_col * tw, 1.0, 0.0).astype(BF16)
    r_i = lax.broadcasted_iota(jnp.int32, (tw, tw), 0)
    c_i = lax.broadcasted_iota(jnp.int32, (tw, tw), 1)
    strict_upper = jnp.where(r_i < c_i, 1.0, 0.0).astype(BF16)

    def prefix(mask):
        mb = jnp.where(mask, 1.0, 0.0).astype(BF16)
        starts = _dot(mb, before_window)
        parts = [_dot(mb[:, w * tw:(w + 1) * tw], strict_upper) + starts[:, w:w + 1] for w in range(T // tw)]
        return jnp.concatenate(parts, axis=1) if len(parts) > 1 else parts[0], starts

    gt = bits > thr
    eq = bits == thr
    need = cap - count(gt)
    tie_rank, _ = prefix(eq)
    sel = gt | (eq & (tie_rank < need))
    slot, starts = prefix(sel)
    pos_ref[0] = jnp.where(sel, slot.astype(jnp.int32), -1)
    gate_ref[0] = jnp.where(sel, aff, 0.0)
    cnt_ref[0] = starts.astype(jnp.int32)


def expert_topk(logits, cap):
    B, E, T = logits.shape
    tw = min(TOKEN_WINDOW, T)
    kern = functools.partial(_topk_kernel, T=T, cap=cap, tw=tw)
    spec = pl.BlockSpec((1, E, T), lambda b: (b, 0, 0))
    return pl.pallas_call(
        kern,
        grid=(B,),
        in_specs=[spec],
        out_specs=[spec, spec, pl.BlockSpec((1, E, V7X_LANES), lambda b: (b, 0, 0))],
        out_shape=[jax.ShapeDtypeStruct((B, E, T), jnp.int32), jax.ShapeDtypeStruct((B, E, T), F32),
                   jax.ShapeDtypeStruct((B, E, V7X_LANES), jnp.int32)],
        compiler_params=_params("arbitrary"),
        name="expert_topk",
    )(logits)


def _slot_span(cnt_ref, idx, C, sp):
    lo = cnt_ref[idx]
    hi = cnt_ref[idx + 1]
    start = jnp.minimum((lo // BF16_ROWS) * BF16_ROWS, C - sp)
    n_extra = jnp.maximum(hi - start - 1, 0) // sp
    return pl.multiple_of(start, BF16_ROWS), n_extra


def _gather_kernel(cnt_ref, pos_ref, h_ref, xs_ref, *, C, sp, nw, E):
    b = pl.program_id(0)
    w = pl.program_id(2)

    @pl.when(w == 0)
    def _():
        xs_ref[...] = jnp.zeros(xs_ref.shape, BF16)

    hwin = h_ref[0]
    tw = hwin.shape[0]
    slot_i = lax.broadcasted_iota(jnp.int32, (sp, tw), 0)
    spans = [_slot_span(cnt_ref, (b * E + e) * (nw + 1) + w, C, sp) for e in range(E)]
    onehots = [jnp.where(pos_ref[0, e:e + 1, :] == slot_i + spans[e][0], 1.0, 0.0).astype(BF16)
               for e in range(E)]
    rows = _dot(jnp.concatenate(onehots, axis=0), hwin)
    for e in range(E):
        start, n_extra = spans[e]
        xs_ref[0, e, pl.ds(start, sp), :] += rows[e * sp:(e + 1) * sp].astype(BF16)

        def extra(i, carry):
            j0 = start + sp * (i + 1)
            j0c = pl.multiple_of(jnp.minimum(j0, C - sp), BF16_ROWS)
            slot = slot_i + j0c
            hit = (pos_ref[0, e:e + 1, :] == slot) & (slot >= j0)
            xs_ref[0, e, pl.ds(j0c, sp), :] += _dot(jnp.where(hit, 1.0, 0.0).astype(BF16), hwin).astype(BF16)
            return carry

        lax.fori_loop(0, n_extra, extra, 0)


def gather_tokens(h, pos, cnt_flat, C):
    B, T, D = h.shape
    E = pos.shape[1]
    tw = min(TOKEN_WINDOW, T)
    nw = T // tw
    sp = min(SLOT_SPAN, C)
    dc = _tile(D, 1024)
    kern = functools.partial(_gather_kernel, C=C, sp=sp, nw=nw, E=E)
    grid_spec = pltpu.PrefetchScalarGridSpec(
        num_scalar_prefetch=1,
        grid=(B, D // dc, nw),
        in_specs=[
            pl.BlockSpec((1, E, tw), lambda b, d, w, cnt: (b, 0, w)),
            pl.BlockSpec((1, tw, dc), lambda b, d, w, cnt: (b, w, d)),
        ],
        out_specs=pl.BlockSpec((1, E, C, dc), lambda b, d, w, cnt: (b, 0, 0, d)),
    )
    return pl.pallas_call(
        kern,
        grid_spec=grid_spec,
        out_shape=jax.ShapeDtypeStruct((B, E, C, D), BF16),
        compiler_params=_params("arbitrary", "arbitrary", "arbitrary"),
        name="moe_gather",
    )(cnt_flat, pos, h)


def _ffn_kernel(*refs, F, n):
    xs_refs, (wgu_ref, wd_ref), y_refs = refs[:n], refs[n:n + 2], refs[n + 2:]
    x = xs_refs[0][0, 0] if n == 1 else jnp.concatenate([r[0, 0] for r in xs_refs], axis=0)
    gu = _dot(x, wgu_ref[0, 0])
    g = gu[:, :F]
    u = gu[:, F:]
    act = (g * _sigmoid(g) * u).astype(BF16)
    y = _dot(act, wd_ref[0, 0]).astype(BF16)
    row = 0
    for r in y_refs:
        rows = r.shape[2]
        r[0, 0] = y[row:row + rows]
        row += rows


def expert_ffn(xs_list, w_gate_up, w_down, l):
    B, E, _, D = xs_list[0].shape
    F = w_down.shape[2]
    slot_spec = lambda xs: pl.BlockSpec((1, 1, xs.shape[2], D), lambda e, b: (b, e, 0, 0))
    kern = functools.partial(_ffn_kernel, F=F, n=len(xs_list))
    return pl.pallas_call(
        kern,
        grid=(E, B),
        in_specs=[slot_spec(xs) for xs in xs_list] + [
            pl.BlockSpec((1, 1, D, 2 * F), lambda e, b: (l, e, 0, 0)),
            pl.BlockSpec((1, 1, F, D), lambda e, b: (l, e, 0, 0)),
        ],
        out_specs=[slot_spec(xs) for xs in xs_list],
        out_shape=[jax.ShapeDtypeStruct(xs.shape, BF16) for xs in xs_list],
        compiler_params=_params("arbitrary", "arbitrary"),
        name="expert_ffn",
    )(*xs_list, w_gate_up, w_down)


def _scatter_kernel(cnt_ref, post_ref, gatet_ref, y_ref, o_ref, acc_ref, *, C, sp, nw, E):
    b = pl.program_id(0)
    w = pl.program_id(2)
    tw = post_ref.shape[1]
    lane = lax.broadcasted_iota(jnp.int32, (tw, 2 * sp), 1)
    lane_slot = lane & (sp - 1)
    first_half = lane < sp

    def gated_onehot(e, j0c, j0):
        pcol = post_ref[0, :, e:e + 1]
        g = gatet_ref[0, :, e:e + 1]
        g_hi = g.astype(BF16).astype(F32)
        slot = lane_slot + j0c
        hit = pcol == slot
        if j0 is not None:
            hit = hit & (slot >= j0)
        return jnp.where(hit, jnp.where(first_half, g_hi, g - g_hi), 0.0).astype(BF16)

    def contribution(experts, starts, j0):
        lhs = [gated_onehot(e, s, j0) for e, s in zip(experts, starts)]
        rhs = []
        for e, s in zip(experts, starts):
            ysp = y_ref[0, e, pl.ds(s, sp), :]
            rhs += [ysp, ysp]
        return _dot(lhs[0] if len(lhs) == 1 else jnp.concatenate(lhs, axis=1), jnp.concatenate(rhs, axis=0))

    spans = [_slot_span(cnt_ref, (b * E + e) * (nw + 1) + w, C, sp) for e in range(E)]
    total = None
    for e in range(0, E, 2):
        r = contribution((e, e + 1), (spans[e][0], spans[e + 1][0]), None)
        total = r if total is None else total + r
    acc_ref[...] = total
    for e in range(E):
        start, n_extra = spans[e]

        def extra(i, carry):
            j0 = start + sp * (i + 1)
            j0c = pl.multiple_of(jnp.minimum(j0, C - sp), BF16_ROWS)
            acc_ref[...] += contribution((e,), (j0c,), j0)
            return carry

        lax.fori_loop(0, n_extra, extra, 0)
    o_ref[0] = acc_ref[...].astype(o_ref.dtype)


def scatter_tokens(y, pos_t, gate_t, cnt_flat):
    B, E, C, D = y.shape
    T = pos_t.shape[1]
    tw = min(TOKEN_WINDOW, T)
    nw = T // tw
    sp = min(SLOT_SPAN, C)
    dc = _tile(D, 1024)
    kern = functools.partial(_scatter_kernel, C=C, sp=sp, nw=nw, E=E)
    grid_spec = pltpu.PrefetchScalarGridSpec(
        num_scalar_prefetch=1,
        grid=(B, D // dc, nw),
        in_specs=[
            pl.BlockSpec((1, tw, E), lambda b, d, w, cnt: (b, w, 0)),
            pl.BlockSpec((1, tw, E), lambda b, d, w, cnt: (b, w, 0)),
            pl.BlockSpec((1, E, C, dc), lambda b, d, w, cnt: (b, 0, 0, d)),
        ],
        out_specs=pl.BlockSpec((1, tw, dc), lambda b, d, w, cnt: (b, w, d)),
        scratch_shapes=[pltpu.VMEM((tw, dc), F32)],
    )
    return pl.pallas_call(
        kern,
        grid_spec=grid_spec,
        out_shape=jax.ShapeDtypeStruct((B, T, D), BF16),
        compiler_params=_params("arbitrary", "arbitrary", "arbitrary"),
        name="moe_scatter",
    )(cnt_flat, pos_t, gate_t, y)


def expert_choice_ffn(streams, w_gate_up, w_down, l):
    routed = []
    for h, logits in streams:
        T, E = h.shape[1], logits.shape[1]
        cap = CAP_FACTOR * T // E
        nw = T // min(TOKEN_WINDOW, T)
        pos, gate, cnt = expert_topk(logits, cap)
        cnt_flat = cnt[:, :, :nw + 1].reshape(-1)
        routed.append((gather_tokens(h, pos, cnt_flat, cap), pos, gate, cnt_flat))
    ys = expert_ffn([r[0] for r in routed], w_gate_up, w_down, l)
    return [scatter_tokens(y, jnp.swapaxes(pos, 1, 2), jnp.swapaxes(gate, 1, 2), cnt_flat)
            for y, (_, pos, gate, cnt_flat) in zip(ys, routed)]


def _rope_tables(T):
    rows = T // GRID_W
    row = jnp.repeat(jnp.arange(rows, dtype=F32), GRID_W)
    col = jnp.tile(jnp.arange(GRID_W, dtype=F32), rows)
    n_freq = HEAD_DIM // 4
    inv = ROPE_THETA ** (-jnp.arange(n_freq, dtype=F32) / n_freq)
    ang = jnp.concatenate([row[:, None] * inv, col[:, None] * inv], axis=-1)
    cos2 = jnp.repeat(jnp.cos(ang), 2, axis=-1)
    sin = jnp.sin(ang)
    sin2 = jnp.stack([-sin, sin], axis=-1).reshape(T, HEAD_DIM)
    return cos2, sin2


def kernel(x, c, ctx, c_ctx, w_ada_dn, w_ada_up, b_ada, w_in, q_norm_g, k_norm_g, conv_w, conv_b, w_rg, b_rg, w_ig, b_ig, lru_lambda, w_branch, w_out, ln_g, ln_b, w_router, w_gate_up, w_down):
    B, T, D = x.shape
    Tc = ctx.shape[1]
    depth = w_in.shape[0]
    in_w = w_in.shape[2]
    lru_w = conv_w.shape[2]
    kvw = (in_w - 2 * lru_w - 2 * D) // (GQA_GROUP + 2)
    qw = GQA_GROUP * kvw
    q_off = 2 * lru_w
    ga_off = q_off + qw + 2 * kvw
    alpha = (2.0 * depth) ** 0.25

    n_rows = -(-(B + 1) // V7X_SUBLANES) * V7X_SUBLANES
    cv = jnp.zeros((n_rows, D), F32).at[:B].set(c).at[B].set(c_ctx)
    mods = ada_modulation(cv, w_ada_dn, w_ada_up, b_ada).reshape(depth, n_rows, 6, D)
    mod_l = [mods[l, :B] for l in range(depth)]
    mod_c = [mods[l, B:B + 1] for l in range(depth)]

    cos2, sin2 = _rope_tables(T)
    w_in_b = w_in.astype(BF16)
    w_branch_b = w_branch.astype(BF16)
    w_out_b = w_out.astype(BF16)
    w_rg_b = w_rg.astype(BF16)
    w_ig_b = w_ig.astype(BF16)
    w_gu_b = w_gate_up.astype(BF16)
    w_dn_b = w_down.astype(BF16)
    w_router_t = jnp.swapaxes(w_router, 1, 2)

    xl, xc = x, ctx
    hl = modulate_rows(xl, mod_l[0])
    hc = modulate_rows(xc, mod_c[0])
    for l in range(depth):
        last = l == depth - 1
        proj_l = matmul(hl.reshape(B * T, D), w_in_b, l).reshape(B, T, in_w)
        proj_c = matmul(hc.reshape(B * Tc, D), w_in_b, l).reshape(B, Tc, in_w)
        q_l, k_l, v_l = norm_rope(proj_l, q_off, qw, kvw, q_norm_g[l], k_norm_g[l], cos2, sin2)
        q_c, k_c, v_c = norm_rope(proj_c, q_off, qw, kvw, q_norm_g[l], k_norm_g[l], None, None)
        k_all = jnp.concatenate([k_c, k_l], axis=1)
        v_all = jnp.concatenate([v_c, v_l], axis=1)
        att_l = attention(q_l, k_all, v_all)
        lru_l, lru_c = lru_branch(proj_l, proj_c, lru_w, l, conv_w, conv_b, w_rg_b, b_rg, w_ig_b, b_ig, lru_lambda)

        def mixer_tail(xs, hs_proj, att, lru, mod, Ts):
            m = merge_branches(att.reshape(B * Ts, qw), lru.reshape(B * Ts, lru_w), hs_proj.reshape(B * Ts, in_w),
                               ga_off, D, w_branch_b, l)
            mix = matmul(m, w_out_b, l).reshape(B, Ts, D)
            return post_ln(xs, mix, mod, mod, ln_g[l, 0], ln_b[l, 0], alpha, 2, 3, 4, w_router_t[l])

        xl1, hl2, logits_l = mixer_tail(xl, proj_l, att_l, lru_l, mod_l[l], T)
        if last:
            (moe_l,) = expert_choice_ffn([(hl2, logits_l)], w_gu_b, w_dn_b, l)
            (xl,) = post_ln(xl1, moe_l, mod_l[l], mod_l[l], ln_g[l, 1], ln_b[l, 1], alpha, 5, 0, 1, emit_h=False)
        else:
            att_c = attention(q_c, k_c, v_c)
            xc1, hc2, logits_c = mixer_tail(xc, proj_c, att_c, lru_c, mod_c[l], Tc)
            moe_l, moe_c = expert_choice_ffn([(hl2, logits_l), (hc2, logits_c)], w_gu_b, w_dn_b, l)
            xl, hl = post_ln(xl1, moe_l, mod_l[l], mod_l[l + 1], ln_g[l, 1], ln_b[l, 1], alpha, 5, 0, 1)
            xc, hc = post_ln(xc1, moe_c, mod_c[l], mod_c[l + 1], ln_g[l, 1], ln_b[l, 1], alpha, 5, 0, 1)
    return xl
```

```python
import functools
import math

import jax
import jax.numpy as jnp
from jax import lax
from jax.experimental import pallas as pl
from jax.experimental.pallas import tpu as pltpu

HEAD_DIM = 128
GQA_GROUP = 4
GRID_W = 64
ROPE_THETA = 10000.0
LRU_BLOCK = 128
LRU_C = 8.0
CONV_W = 4
CONV_LEFT = 2
N_EXPERTS = 16
CAP_FACTOR = 2
NORM_EPS = 1e-6
LOG2_E = 1.4426950408889634
GELU_K0 = 0.7978845608028654
GELU_K1 = 0.044715
TINY_F32 = 1e-30
LRU_SCAN_UNROLL = 4

V7X_LANES = 128
V7X_SUBLANES = 8
V7X_VMEM_LIMIT_BYTES = 56 * 1024 * 1024
V7X_MXU_DEPTH = 256
FFN_GATE_ROWS = 64

ATTN_TQ = 512
ATTN_TK = 1024
ATTN_SUB = 128
TOKEN_WINDOW = 256
SLOT_SPAN = 64
BF16_ROWS = 16

F32 = jnp.float32
BF16 = jnp.bfloat16
HIGHEST = lax.Precision.HIGHEST


def _tile(n, pref):
    t = min(pref, n)
    while n % t:
        t //= 2
    return t


def _params(*sem):
    return pltpu.CompilerParams(dimension_semantics=sem, vmem_limit_bytes=V7X_VMEM_LIMIT_BYTES)


def _sigmoid(x):
    return 1.0 / (1.0 + jnp.exp(-x))


def _dot(a, b):
    return jnp.dot(a, b, preferred_element_type=F32)


def _ada_kernel(c_ref, wdn_ref, wup_ref, b_ref, o_ref):
    c = c_ref[...]
    s = c * _sigmoid(c)
    t = jnp.dot(s, wdn_ref[0], preferred_element_type=F32, precision=HIGHEST)
    o_ref[0] = jnp.dot(t, wup_ref[0], preferred_element_type=F32, precision=HIGHEST) + b_ref[0]


def ada_modulation(cv, w_dn, w_up, b_up):
    L, D, R = w_dn.shape
    N = w_up.shape[2]
    rows = cv.shape[0]
    tn = _tile(N, 4096)
    return pl.pallas_call(
        _ada_kernel,
        grid=(L, N // tn),
        in_specs=[
            pl.BlockSpec((rows, D), lambda l, j: (0, 0)),
            pl.BlockSpec((1, D, R), lambda l, j: (l, 0, 0)),
            pl.BlockSpec((1, R, tn), lambda l, j: (l, 0, j)),
            pl.BlockSpec((1, 1, tn), lambda l, j: (l, 0, j)),
        ],
        out_specs=pl.BlockSpec((1, rows, tn), lambda l, j: (l, 0, j)),
        out_shape=jax.ShapeDtypeStruct((L, rows, N), F32),
        compiler_params=_params("arbitrary", "arbitrary"),
        name="ada_modulation",
    )(cv, w_dn, w_up, b_up.reshape(L, 1, N))


def _modulate_kernel(x_ref, mod_ref, h_ref):
    sh = mod_ref[0, 0:1, :]
    sc = mod_ref[0, 1:2, :]
    h_ref[0] = (x_ref[0] * (1.0 + sc) + sh).astype(BF16)


def modulate_rows(x, mod):
    B, T, D = x.shape
    tr = _tile(T, 512)
    per_batch = mod.shape[0] > 1
    return pl.pallas_call(
        _modulate_kernel,
        grid=(B, T // tr),
        in_specs=[
            pl.BlockSpec((1, tr, D), lambda b, i: (b, i, 0)),
            pl.BlockSpec((1, 6, D), (lambda b, i: (b, 0, 0)) if per_batch else (lambda b, i: (0, 0, 0))),
        ],
        out_specs=pl.BlockSpec((1, tr, D), lambda b, i: (b, i, 0)),
        out_shape=jax.ShapeDtypeStruct((B, T, D), BF16),
        compiler_params=_params("arbitrary", "arbitrary"),
        name="modulate",
    )(x, mod)


def _matmul_kernel(a_ref, w_ref, o_ref):
    o_ref[...] = _dot(a_ref[...], w_ref[0]).astype(o_ref.dtype)


def matmul(a, w, l, out_dtype=BF16):
    M, K = a.shape
    N = w.shape[2]
    tm, tn = _tile(M, 1024), _tile(N, 1024)
    return pl.pallas_call(
        _matmul_kernel,
        grid=(M // tm, N // tn),
        in_specs=[
            pl.BlockSpec((tm, K), lambda i, j: (i, 0)),
            pl.BlockSpec((1, K, tn), lambda i, j: (l, 0, j)),
        ],
        out_specs=pl.BlockSpec((tm, tn), lambda i, j: (i, j)),
        out_shape=jax.ShapeDtypeStruct((M, N), out_dtype),
        compiler_params=_params("arbitrary", "arbitrary"),
        name="matmul",
    )(a, w)


def _normrope_kernel(*refs, nq, nkv, rope, scale):
    if rope:
        q_ref, kv_ref, qg_ref, kg_ref, cos_ref, sin_ref, qo_ref, ko_ref, vo_ref = refs
        cs = cos_ref[...]
        sn = sin_ref[...]
        lane = lax.broadcasted_iota(jnp.int32, cs.shape, 1)
        even = (lane & 1) == 0
    else:
        q_ref, kv_ref, qg_ref, kg_ref, qo_ref, ko_ref, vo_ref = refs

    def head(xh, g):
        xf = xh.astype(F32)
        y = xf * lax.rsqrt(jnp.mean(xf * xf, axis=-1, keepdims=True) + NORM_EPS) * g
        if rope:
            partner = jnp.where(even, pltpu.roll(y, HEAD_DIM - 1, 1), pltpu.roll(y, 1, 1))
            y = y * cs + partner * sn
        return y

    qg = qg_ref[...]
    kg = kg_ref[...]
    for h in range(nq):
        sl = slice(h * HEAD_DIM, (h + 1) * HEAD_DIM)
        qo_ref[0, :, sl] = (head(q_ref[0, :, sl], qg) * scale).astype(BF16)
    for h in range(nkv):
        sl = slice(h * HEAD_DIM, (h + 1) * HEAD_DIM)
        ko_ref[0, :, sl] = head(kv_ref[0, :, sl], kg).astype(BF16)
    vo_ref[0] = kv_ref[0, :, nkv * HEAD_DIM:]


def norm_rope(proj, q_off, qw, kvw, qg, kg, cos2, sin2):
    B, T, _ = proj.shape
    rope = cos2 is not None
    tr = _tile(T, 512)
    assert q_off % qw == 0 and (q_off + qw) % (2 * kvw) == 0
    qb, kvb = q_off // qw, (q_off + qw) // (2 * kvw)
    in_specs = [
        pl.BlockSpec((1, tr, qw), lambda b, i: (b, i, qb)),
        pl.BlockSpec((1, tr, 2 * kvw), lambda b, i: (b, i, kvb)),
        pl.BlockSpec((1, HEAD_DIM), lambda b, i: (0, 0)),
        pl.BlockSpec((1, HEAD_DIM), lambda b, i: (0, 0)),
    ]
    args = [proj, proj, qg.reshape(1, HEAD_DIM), kg.reshape(1, HEAD_DIM)]
    if rope:
        in_specs += [pl.BlockSpec((tr, HEAD_DIM), lambda b, i: (i, 0))] * 2
        args += [cos2, sin2]
    kern = functools.partial(_normrope_kernel, nq=qw // HEAD_DIM, nkv=kvw // HEAD_DIM, rope=rope,
                             scale=HEAD_DIM ** -0.5)
    return pl.pallas_call(
        kern,
        grid=(B, T // tr),
        in_specs=in_specs,
        out_specs=[
            pl.BlockSpec((1, tr, qw), lambda b, i: (b, i, 0)),
            pl.BlockSpec((1, tr, kvw), lambda b, i: (b, i, 0)),
            pl.BlockSpec((1, tr, kvw), lambda b, i: (b, i, 0)),
        ],
        out_shape=[
            jax.ShapeDtypeStruct((B, T, qw), BF16),
            jax.ShapeDtypeStruct((B, T, kvw), BF16),
            jax.ShapeDtypeStruct((B, T, kvw), BF16),
        ],
        compiler_params=_params("arbitrary", "arbitrary"),
        name="norm_rope",
    )(*args)


def _attn_kernel(q_ref, k_ref, v_ref, o_ref, vx_ref, m_ref, acc_ref, *, tq, first, tk, n_chunks):
    @pl.when(pl.program_id(2) == 0)
    def _():
        vx_ref[:, 0:HEAD_DIM] = v_ref[0]
        vx_ref[:, HEAD_DIM:] = jnp.ones((vx_ref.shape[0], HEAD_DIM), BF16)

    q = jnp.concatenate([q_ref[0, :, g * HEAD_DIM:(g + 1) * HEAD_DIM] for g in range(GQA_GROUP)], axis=0)
    m_ref[...] = jnp.full(m_ref.shape, -jnp.inf, F32)
    acc_ref[...] = jnp.zeros(acc_ref.shape, F32)

    def chunk(off, size):
        k = k_ref[0, pl.ds(off, size), :]
        vx = vx_ref[pl.ds(off, size), :]
        for r in range(0, GQA_GROUP * tq, ATTN_SUB):
            rows = slice(r, r + ATTN_SUB)
            s = lax.dot_general(q[rows], k, (((1,), (1,)), ((), ())), preferred_element_type=F32)
            m_prev = m_ref[rows, :]
            m_next = jnp.maximum(m_prev, jnp.max(s, axis=-1, keepdims=True))
            alpha = jnp.exp(m_prev - m_next)
            p = jnp.exp(s - jnp.concatenate([m_next] * (size // HEAD_DIM), axis=1))
            acc_ref[rows, :] = jnp.concatenate([alpha, alpha], axis=1) * acc_ref[rows, :] + _dot(p.astype(BF16), vx)
            m_ref[rows, :] = m_next

    chunk(0, first)
    for i in range(n_chunks):
        chunk(first + i * tk, tk)
    acc = acc_ref[...]
    o = acc[:, :HEAD_DIM] * (1.0 / acc[:, HEAD_DIM:])
    for g in range(GQA_GROUP):
        o_ref[0, :, g * HEAD_DIM:(g + 1) * HEAD_DIM] = o[g * tq:(g + 1) * tq].astype(BF16)


def attention(q, k, v):
    B, Tq, QW = q.shape
    Tk, KVW = k.shape[1], k.shape[2]
    nkv = KVW // HEAD_DIM
    tq = _tile(Tq, ATTN_TQ)
    tk = min(ATTN_TK, Tk)
    n_chunks = (Tk - 1) // tk
    first = Tk - n_chunks * tk
    assert first % HEAD_DIM == 0 and tk % HEAD_DIM == 0
    gw = GQA_GROUP * HEAD_DIM
    kern = functools.partial(_attn_kernel, tq=tq, first=first, tk=tk, n_chunks=n_chunks)
    return pl.pallas_call(
        kern,
        grid=(B, nkv, Tq // tq),
        in_specs=[
            pl.BlockSpec((1, tq, gw), lambda b, h, i: (b, i, h)),
            pl.BlockSpec((1, Tk, HEAD_DIM), lambda b, h, i: (b, 0, h)),
            pl.BlockSpec((1, Tk, HEAD_DIM), lambda b, h, i: (b, 0, h)),
        ],
        out_specs=pl.BlockSpec((1, tq, gw), lambda b, h, i: (b, i, h)),
        out_shape=jax.ShapeDtypeStruct((B, Tq, QW), BF16),
        scratch_shapes=[
            pltpu.VMEM((Tk, 2 * HEAD_DIM), BF16),
            pltpu.VMEM((GQA_GROUP * tq, HEAD_DIM), F32),
            pltpu.VMEM((GQA_GROUP * tq, 2 * HEAD_DIM), F32),
        ],
        compiler_params=_params("arbitrary", "arbitrary", "arbitrary"),
        name="attention",
    )(q, k, v)


def _gelu_tanh(x):
    half = 0.5 * x
    return half + half * jnp.tanh(x * (GELU_K0 + (GELU_K0 * GELU_K1) * (x * x)))


def _lru_kernel(xl_ref, gl_ref, xc_ref, gc_ref, cw_ref, cb_ref, wr_ref, br_ref, wi_ref, bi_ref, lam_ref,
                yl_ref, yc_ref, pad_ref, al_ref, bl_ref, hl_ref, ac_ref, bc_ref, hc_ref, *, Ll, Lc):
    S = V7X_SUBLANES
    cw = cw_ref[0]
    cb = cb_ref[0]

    def coeffs(x_ref, L, a_ref, b_ref):
        pad_ref[0:S, :] = jnp.zeros((S, LRU_BLOCK), F32)
        pad_ref[S:S + L, :] = x_ref[0].astype(F32)
        pad_ref[S + L:2 * S + L, :] = jnp.zeros((S, LRU_BLOCK), F32)
        u = cb
        for j in range(CONV_W):
            start = S - CONV_LEFT + j
            u = u + pad_ref[start:start + L, :] * cw[j:j + 1, :]
        ub = u.astype(BF16)
        half_u = 0.5 * u
        for d in range(2):
            tr = jnp.tanh(0.5 * (_dot(ub, wr_ref[0, d, 0]) + br_ref[0, d:d + 1, :]))
            ti = jnp.tanh(0.5 * (_dot(ub, wi_ref[0, d, 0]) + bi_ref[0, d:d + 1, :]))
            nl = -lam_ref[0, d:d + 1, :]
            softplus = jnp.maximum(nl, 0.0) + jnp.log(1.0 + jnp.exp(-jnp.abs(nl)))
            k = (-0.5 * LRU_C * LOG2_E) * softplus
            a = jnp.exp2(k + k * tr)
            a_ref[d] = a
            gap = 1.0 - a * a
            b_ref[d] = gap * lax.rsqrt(jnp.maximum(gap, TINY_F32)) * (half_u + half_u * ti)

    row = lax.broadcasted_iota(jnp.int32, (S, LRU_BLOCK), 0)

    def tile_scan(a, b, carry, reverse):
        for d in (1, 2, 4):
            if reverse:
                keep = row < S - d
                shift = S - d
            else:
                keep = row >= d
                shift = d
            b = b + a * jnp.where(keep, pltpu.roll(b, shift, 0), 0.0)
            a = a * jnp.where(keep, pltpu.roll(a, shift, 0), 1.0)
        h = b + a * carry
        last = h[0:1, :] if reverse else h[S - 1:S, :]
        return h, jnp.broadcast_to(last, (S, LRU_BLOCK))

    def scan(L, a_ref, b_ref, h_ref, carry_f, carry_r):
        n = L // S

        def body(s, carry):
            cf, cr = carry
            for j in range(LRU_SCAN_UNROLL):
                rf = pl.ds(pl.multiple_of((s * LRU_SCAN_UNROLL + j) * S, S), S)
                hf, cf = tile_scan(a_ref[0, rf, :], b_ref[0, rf, :], cf, False)
                h_ref[0, rf, :] = hf
                rr = pl.ds(pl.multiple_of((n - 1 - s * LRU_SCAN_UNROLL - j) * S, S), S)
                hr, cr = tile_scan(a_ref[1, rr, :], b_ref[1, rr, :], cr, True)
                h_ref[1, rr, :] = hr
            return cf, cr

        return lax.fori_loop(0, n // LRU_SCAN_UNROLL, body, (carry_f, carry_r))

    def finish(h_ref, g_ref, y_ref):
        y_ref[0] = ((h_ref[0] + h_ref[1]) * _gelu_tanh(g_ref[0].astype(F32))).astype(BF16)

    coeffs(xc_ref, Lc, ac_ref, bc_ref)
    coeffs(xl_ref, Ll, al_ref, bl_ref)
    zero = jnp.zeros((S, LRU_BLOCK), F32)
    end_f, end_r = scan(Lc, ac_ref, bc_ref, hc_ref, zero, zero)
    scan(Ll, al_ref, bl_ref, hl_ref, end_f, end_r)
    finish(hc_ref, gc_ref, yc_ref)
    finish(hl_ref, gl_ref, yl_ref)


def lru_branch(proj_l, proj_c, lru_w, l, conv_w, conv_b, w_rg, b_rg, w_ig, b_ig, lam):
    B, Ll, _ = proj_l.shape
    Lc = proj_c.shape[1]
    nb = lru_w // LRU_BLOCK
    S = V7X_SUBLANES
    assert Ll % (S * LRU_SCAN_UNROLL) == 0 and Lc % (S * LRU_SCAN_UNROLL) == 0
    blk = lambda L, off: pl.BlockSpec((1, L, LRU_BLOCK), lambda b, h: (b, 0, h + off))
    chan = lambda rows: pl.BlockSpec((1, rows, LRU_BLOCK), lambda b, h: (l, 0, h))
    wspec = pl.BlockSpec((1, 2, 1, LRU_BLOCK, LRU_BLOCK), lambda b, h: (l, 0, h, 0, 0))
    kern = functools.partial(_lru_kernel, Ll=Ll, Lc=Lc)
    return pl.pallas_call(
        kern,
        grid=(B, nb),
        in_specs=[blk(Ll, 0), blk(Ll, nb), blk(Lc, 0), blk(Lc, nb), chan(CONV_W), chan(1),
                  wspec, chan(2), wspec, chan(2), chan(2)],
        out_specs=[pl.BlockSpec((1, Ll, LRU_BLOCK), lambda b, h: (b, 0, h)),
                   pl.BlockSpec((1, Lc, LRU_BLOCK), lambda b, h: (b, 0, h))],
        out_shape=[jax.ShapeDtypeStruct((B, Ll, lru_w), BF16), jax.ShapeDtypeStruct((B, Lc, lru_w), BF16)],
        scratch_shapes=[pltpu.VMEM((Ll + 2 * S, LRU_BLOCK), F32)]
        + [pltpu.VMEM((2, Ll, LRU_BLOCK), F32)] * 3 + [pltpu.VMEM((2, Lc, LRU_BLOCK), F32)] * 3,
        compiler_params=_params("arbitrary", "arbitrary"),
        name="rglru",
    )(proj_l, proj_l, proj_c, proj_c, conv_w, conv_b.reshape(conv_b.shape[0], 1, lru_w), w_rg, b_rg, w_ig, b_ig, lam)


def _merge_kernel(att_ref, lru_ref, ga_ref, gl_ref, wa_ref, wl_ref, o_ref):
    ya = _dot(att_ref[...], wa_ref[0, 0])
    yl = _dot(lru_ref[...], wl_ref[0, 0])
    m = _sigmoid(ga_ref[...].astype(F32)) * ya + _sigmoid(gl_ref[...].astype(F32)) * yl
    o_ref[...] = m.astype(BF16)


def merge_branches(att, lru, proj, ga_off, D, w_branch, l):
    M, K = att.shape
    tm = _tile(M, 1024)
    tn = 1024
    while ga_off % tn or D % tn:
        tn //= 2
    ga_b, gl_b = ga_off // tn, (ga_off + D) // tn
    return pl.pallas_call(
        _merge_kernel,
        grid=(M // tm, D // tn),
        in_specs=[
            pl.BlockSpec((tm, K), lambda i, j: (i, 0)),
            pl.BlockSpec((tm, K), lambda i, j: (i, 0)),
            pl.BlockSpec((tm, tn), lambda i, j: (i, ga_b + j)),
            pl.BlockSpec((tm, tn), lambda i, j: (i, gl_b + j)),
            pl.BlockSpec((1, 1, K, tn), lambda i, j: (l, 0, 0, j)),
            pl.BlockSpec((1, 1, K, tn), lambda i, j: (l, 1, 0, j)),
        ],
        out_specs=pl.BlockSpec((tm, tn), lambda i, j: (i, j)),
        out_shape=jax.ShapeDtypeStruct((M, D), BF16),
        compiler_params=_params("arbitrary", "arbitrary"),
        name="merge",
    )(att, lru, proj, proj, w_branch, w_branch)


def _postln_kernel(*refs, gate_row, sh_row, sc_row, alpha, router, emit_h):
    x_ref, y_ref, mod_ref, modn_ref, lng_ref, lnb_ref = refs[:6]
    rest = list(refs[6:])
    wr_ref = rest.pop(0) if router else None
    xo_ref = rest.pop(0)
    ho_ref = rest.pop(0) if emit_h else None
    lg_ref = rest.pop(0) if router else None
    z = alpha * x_ref[0] + mod_ref[0, gate_row:gate_row + 1, :] * y_ref[0].astype(F32)
    mu = jnp.mean(z, axis=-1, keepdims=True)
    zc = z - mu
    var = jnp.mean(zc * zc, axis=-1, keepdims=True)
    xn = zc * lax.rsqrt(var + NORM_EPS) * lng_ref[...] + lnb_ref[...]
    xo_ref[0] = xn
    if emit_h:
        h = xn * (1.0 + modn_ref[0, sc_row:sc_row + 1, :]) + modn_ref[0, sh_row:sh_row + 1, :]
        ho_ref[0] = h.astype(BF16)
        if router:
            lg_ref[0] = lax.dot_general(wr_ref[...], h, (((1,), (1,)), ((), ())),
                                        preferred_element_type=F32, precision=HIGHEST)


def post_ln(x, y, mod, modn, ln_g, ln_b, alpha, gate_row, sh_row, sc_row, w_router_t=None, emit_h=True):
    B, T, D = x.shape
    tr = _tile(T, 256)
    router = w_router_t is not None
    mspec = lambda m: pl.BlockSpec((1, 6, D), (lambda b, i: (b, 0, 0)) if m.shape[0] > 1 else (lambda b, i: (0, 0, 0)))
    row = pl.BlockSpec((1, tr, D), lambda b, i: (b, i, 0))
    vec = pl.BlockSpec((1, D), lambda b, i: (0, 0))
    in_specs = [row, row, mspec(mod), mspec(modn), vec, vec]
    args = [x, y, mod, modn, ln_g.reshape(1, D), ln_b.reshape(1, D)]
    out_specs = [row]
    out_shape = [jax.ShapeDtypeStruct((B, T, D), F32)]
    if router:
        E = w_router_t.shape[0]
        in_specs.append(pl.BlockSpec((E, D), lambda b, i: (0, 0)))
        args.append(w_router_t)
    if emit_h:
        out_specs.append(row)
        out_shape.append(jax.ShapeDtypeStruct((B, T, D), BF16))
    if router:
        out_specs.append(pl.BlockSpec((1, E, tr), lambda b, i: (b, 0, i)))
        out_shape.append(jax.ShapeDtypeStruct((B, E, T), F32))
    kern = functools.partial(_postln_kernel, gate_row=gate_row, sh_row=sh_row, sc_row=sc_row, alpha=alpha,
                             router=router, emit_h=emit_h)
    return pl.pallas_call(
        kern,
        grid=(B, T // tr),
        in_specs=in_specs,
        out_specs=out_specs,
        out_shape=out_shape,
        compiler_params=_params("arbitrary", "arbitrary"),
        name="post_ln",
    )(*args)


def _topk_kernel(lg_ref, pos_ref, gate_ref, cnt_ref, *, T, cap, tw):
    lg = lg_ref[0]
    E = lg.shape[0]
    ex = jnp.exp(lg - jnp.max(lg, axis=0, keepdims=True))
    aff = ex / jnp.sum(ex, axis=0, keepdims=True)
    bits = lax.bitcast_convert_type(aff, jnp.int32)

    def count(mask):
        return jnp.sum(jnp.where(mask, 1.0, 0.0), axis=1, keepdims=True)

    thr = jnp.zeros((E, 1), jnp.int32)
    for bit in range(30, -1, -1):
        cand = thr | (1 << bit)
        thr = jnp.where(count(bits >= cand) >= cap, cand, thr)

    t_row = lax.broadcasted_iota(jnp.int32, (T, V7X_LANES), 0)
    w_col = lax.broadcasted_iota(jnp.int32, (T, V7X_LANES), 1)
    before_window = jnp.where(t_row < w_col * tw, 1.0, 0.0).astype(BF16)
    r_i = lax.broadcasted_iota(jnp.int32, (tw, tw), 0)
    c_i = lax.broadcasted_iota(jnp.int32, (tw, tw), 1)
    strict_upper = jnp.where(r_i < c_i, 1.0, 0.0).astype(BF16)

    def prefix(mask):
        mb = jnp.where(mask, 1.0, 0.0).astype(BF16)
        starts = _dot(mb, before_window)
        parts = [_dot(mb[:, w * tw:(w + 1) * tw], strict_upper) + starts[:, w:w + 1] for w in range(T // tw)]
        return jnp.concatenate(parts, axis=1) if len(parts) > 1 else parts[0], starts

    gt = bits > thr
    eq = bits == thr
    need = cap - count(gt)
    tie_rank, _ = prefix(eq)
    sel = gt | (eq & (tie_rank < need))
    slot, starts = prefix(sel)
    pos_ref[0] = jnp.where(sel, slot.astype(jnp.int32), -1)
    gate_ref[0] = jnp.where(sel, aff, 0.0)
    cnt_ref[0] = starts.astype(jnp.int32)


def expert_topk(logits, cap):
    B, E, T = logits.shape
    tw = min(TOKEN_WINDOW, T)
    kern = functools.partial(_topk_kernel, T=T, cap=cap, tw=tw)
    spec = pl.BlockSpec((1, E, T), lambda b: (b, 0, 0))
    return pl.pallas_call(
        kern,
        grid=(B,),
        in_specs=[spec],
        out_specs=[spec, spec, pl.BlockSpec((1, E, V7X_LANES), lambda b: (b, 0, 0))],
        out_shape=[jax.ShapeDtypeStruct((B, E, T), jnp.int32), jax.ShapeDtypeStruct((B, E, T), F32),
                   jax.ShapeDtypeStruct((B, E, V7X_LANES), jnp.int32)],
        compiler_params=_params("arbitrary"),
        name="expert_topk",
    )(logits)


def _slot_span(cnt_ref, idx, C, sp):
    lo = cnt_ref[idx]
    hi = cnt_ref[idx + 1]
    start = jnp.minimum((lo // BF16_ROWS) * BF16_ROWS, C - sp)
    n_extra = jnp.maximum(hi - start - 1, 0) // sp
    return pl.multiple_of(start, BF16_ROWS), n_extra


def _gather_kernel(cnt_ref, pos_ref, h_ref, xs_ref, *, C, sp, nw, E):
    b = pl.program_id(0)
    w = pl.program_id(2)

    @pl.when(w == 0)
    def _():
        xs_ref[...] = jnp.zeros(xs_ref.shape, BF16)

    hwin = h_ref[0]
    tw = hwin.shape[0]
    slot_i = lax.broadcasted_iota(jnp.int32, (sp, tw), 0)
    spans = [_slot_span(cnt_ref, (b * E + e) * (nw + 1) + w, C, sp) for e in range(E)]
    onehots = [jnp.where(pos_ref[0, e:e + 1, :] == slot_i + spans[e][0], 1.0, 0.0).astype(BF16)
               for e in range(E)]
    rows = _dot(jnp.concatenate(onehots, axis=0), hwin)
    for e in range(E):
        start, n_extra = spans[e]
        xs_ref[0, e, pl.ds(start, sp), :] += rows[e * sp:(e + 1) * sp].astype(BF16)

    @pl.when(sum(n for _, n in spans) > 0)
    def _():
        for e in range(E):
            start, n_extra = spans[e]

            def extra(i, carry):
                j0 = start + sp * (i + 1)
                j0c = pl.multiple_of(jnp.minimum(j0, C - sp), BF16_ROWS)
                slot = slot_i + j0c
                hit = (pos_ref[0, e:e + 1, :] == slot) & (slot >= j0)
                xs_ref[0, e, pl.ds(j0c, sp), :] += _dot(jnp.where(hit, 1.0, 0.0).astype(BF16), h_ref[0]).astype(BF16)
                return carry

            lax.fori_loop(0, n_extra, extra, 0)


def gather_tokens(h, pos, cnt_flat, C):
    B, T, D = h.shape
    E = pos.shape[1]
    tw = min(TOKEN_WINDOW, T)
    nw = T // tw
    sp = min(SLOT_SPAN, C)
    dc = _tile(D, 1024)
    kern = functools.partial(_gather_kernel, C=C, sp=sp, nw=nw, E=E)
    grid_spec = pltpu.PrefetchScalarGridSpec(
        num_scalar_prefetch=1,
        grid=(B, D // dc, nw),
        in_specs=[
            pl.BlockSpec((1, E, tw), lambda b, d, w, cnt: (b, 0, w)),
            pl.BlockSpec((1, tw, dc), lambda b, d, w, cnt: (b, w, d)),
        ],
        out_specs=pl.BlockSpec((1, E, C, dc), lambda b, d, w, cnt: (b, 0, 0, d)),
    )
    return pl.pallas_call(
        kern,
        grid_spec=grid_spec,
        out_shape=jax.ShapeDtypeStruct((B, E, C, D), BF16),
        compiler_params=_params("arbitrary", "arbitrary", "arbitrary"),
        name="moe_gather",
    )(cnt_flat, pos, h)


def _slot_gates(pos_ref, gate_ref, C):
    T = pos_ref.shape[-1]
    rows = min(C, FFN_GATE_ROWS)
    cols = []
    for r0 in range(0, C, rows):
        slot = lax.broadcasted_iota(jnp.int32, (rows, V7X_LANES), 0) + r0
        acc = jnp.zeros((rows, V7X_LANES), F32)
        for t0 in range(0, T, V7X_LANES):
            p = pos_ref[0, 0, :, t0:t0 + V7X_LANES]
            acc = acc + jnp.where(p == slot, gate_ref[0, 0, :, t0:t0 + V7X_LANES], 0.0)
        cols.append(jnp.sum(acc, axis=-1, keepdims=True))
    return cols[0] if len(cols) == 1 else jnp.concatenate(cols, axis=0)


def _ffn_kernel(*refs, F, n):
    ins, (wgu_ref, wd_ref), y_refs = refs[:3 * n], refs[3 * n:3 * n + 2], refs[3 * n + 2:]
    xs_refs, pos_refs, gate_refs = ins[0::3], ins[1::3], ins[2::3]
    x = xs_refs[0][0, 0] if n == 1 else jnp.concatenate([r[0, 0] for r in xs_refs], axis=0)
    gates = [_slot_gates(p, g, r.shape[2]) for r, p, g in zip(xs_refs, pos_refs, gate_refs)]
    gu = _dot(x, wgu_ref[0, 0])
    g = gu[:, :F]
    u = gu[:, F:]
    act = (g * _sigmoid(g) * u).astype(BF16)
    y = _dot(act, wd_ref[0, 0])
    row = 0
    for r, gate in zip(y_refs, gates):
        rows = r.shape[2]
        r[0, 0] = (y[row:row + rows] * gate).astype(BF16)
        row += rows


def expert_ffn(streams, w_gate_up, w_down, l):
    B, E, _, D = streams[0][0].shape
    F = w_down.shape[2]
    slot_spec = lambda xs: pl.BlockSpec((1, 1, xs.shape[2], D), lambda e, b: (b, e, 0, 0))
    tok_spec = lambda T: pl.BlockSpec((1, 1, 1, T), lambda e, b: (b, e, 0, 0))
    in_specs, args = [], []
    for xs, pos, gate in streams:
        T = pos.shape[2]
        in_specs += [slot_spec(xs), tok_spec(T), tok_spec(T)]
        args += [xs, pos.reshape(B, E, 1, T), gate.reshape(B, E, 1, T)]
    kern = functools.partial(_ffn_kernel, F=F, n=len(streams))
    return pl.pallas_call(
        kern,
        grid=(E, B),
        in_specs=in_specs + [
            pl.BlockSpec((1, 1, D, 2 * F), lambda e, b: (l, e, 0, 0)),
            pl.BlockSpec((1, 1, F, D), lambda e, b: (l, e, 0, 0)),
        ],
        out_specs=[slot_spec(xs) for xs, _, _ in streams],
        out_shape=[jax.ShapeDtypeStruct(xs.shape, BF16) for xs, _, _ in streams],
        compiler_params=_params("arbitrary", "arbitrary"),
        name="expert_ffn",
    )(*args, w_gate_up, w_down)


def _scatter_kernel(cnt_ref, post_ref, y_ref, o_ref, acc_ref, *, C, sp, nw, E):
    b = pl.program_id(0)
    w = pl.program_id(2)
    tw = post_ref.shape[1]
    lane_slot = lax.broadcasted_iota(jnp.int32, (tw, sp), 1)

    def onehot(e, j0c, j0):
        slot = lane_slot + j0c
        hit = post_ref[0, :, e:e + 1] == slot
        if j0 is not None:
            hit = hit & (slot >= j0)
        return jnp.where(hit, 1.0, 0.0).astype(BF16)

    def contribution(experts, starts, j0):
        lhs = [onehot(e, s, j0) for e, s in zip(experts, starts)]
        rhs = [y_ref[0, e, pl.ds(s, sp), :] for e, s in zip(experts, starts)]
        if len(lhs) == 1:
            return _dot(lhs[0], rhs[0])
        return _dot(jnp.concatenate(lhs, axis=1), jnp.concatenate(rhs, axis=0))

    spans = [_slot_span(cnt_ref, (b * E + e) * (nw + 1) + w, C, sp) for e in range(E)]
    group = max(1, min(E, V7X_MXU_DEPTH // sp))
    total = None
    for e0 in range(0, E, group):
        es = tuple(range(e0, min(e0 + group, E)))
        r = contribution(es, tuple(spans[e][0] for e in es), None)
        total = r if total is None else total + r
    o_ref[0] = total.astype(o_ref.dtype)

    @pl.when(sum(n for _, n in spans) > 0)
    def _():
        acc_ref[...] = total
        for e in range(E):
            start, n_extra = spans[e]

            def extra(i, carry):
                j0 = start + sp * (i + 1)
                j0c = pl.multiple_of(jnp.minimum(j0, C - sp), BF16_ROWS)
                acc_ref[...] += contribution((e,), (j0c,), j0)
                return carry

            lax.fori_loop(0, n_extra, extra, 0)
        o_ref[0] = acc_ref[...].astype(o_ref.dtype)


def scatter_tokens(y, pos_t, cnt_flat):
    B, E, C, D = y.shape
    T = pos_t.shape[1]
    tw = min(TOKEN_WINDOW, T)
    nw = T // tw
    sp = min(SLOT_SPAN, C)
    dc = _tile(D, 1024)
    kern = functools.partial(_scatter_kernel, C=C, sp=sp, nw=nw, E=E)
    grid_spec = pltpu.PrefetchScalarGridSpec(
        num_scalar_prefetch=1,
        grid=(B, D // dc, nw),
        in_specs=[
            pl.BlockSpec((1, tw, E), lambda b, d, w, cnt: (b, w, 0)),
            pl.BlockSpec((1, E, C, dc), lambda b, d, w, cnt: (b, 0, 0, d)),
        ],
        out_specs=pl.BlockSpec((1, tw, dc), lambda b, d, w, cnt: (b, w, d)),
        scratch_shapes=[pltpu.VMEM((tw, dc), F32)],
    )
    return pl.pallas_call(
        kern,
        grid_spec=grid_spec,
        out_shape=jax.ShapeDtypeStruct((B, T, D), BF16),
        compiler_params=_params("arbitrary", "arbitrary", "arbitrary"),
        name="moe_scatter",
    )(cnt_flat, pos_t, y)


def expert_choice_ffn(streams, w_gate_up, w_down, l):
    routed = []
    for h, logits in streams:
        T, E = h.shape[1], logits.shape[1]
        cap = CAP_FACTOR * T // E
        nw = T // min(TOKEN_WINDOW, T)
        pos, gate, cnt = expert_topk(logits, cap)
        cnt_flat = cnt[:, :, :nw + 1].reshape(-1)
        routed.append((gather_tokens(h, pos, cnt_flat, cap), pos, gate, cnt_flat))
    ys = expert_ffn([(xs, pos, gate) for xs, pos, gate, _ in routed], w_gate_up, w_down, l)
    return [scatter_tokens(y, jnp.swapaxes(pos, 1, 2), cnt_flat) for y, (_, pos, _, cnt_flat) in zip(ys, routed)]


def _rope_tables(T):
    rows = T // GRID_W
    row = jnp.repeat(jnp.arange(rows, dtype=F32), GRID_W)
    col = jnp.tile(jnp.arange(GRID_W, dtype=F32), rows)
    n_freq = HEAD_DIM // 4
    inv = ROPE_THETA ** (-jnp.arange(n_freq, dtype=F32) / n_freq)
    ang = jnp.concatenate([row[:, None] * inv, col[:, None] * inv], axis=-1)
    cos2 = jnp.repeat(jnp.cos(ang), 2, axis=-1)
    sin = jnp.sin(ang)
    sin2 = jnp.stack([-sin, sin], axis=-1).reshape(T, HEAD_DIM)
    return cos2, sin2


def kernel(x, c, ctx, c_ctx, w_ada_dn, w_ada_up, b_ada, w_in, q_norm_g, k_norm_g, conv_w, conv_b, w_rg, b_rg, w_ig, b_ig, lru_lambda, w_branch, w_out, ln_g, ln_b, w_router, w_gate_up, w_down):
    B, T, D = x.shape
    Tc = ctx.shape[1]
    depth = w_in.shape[0]
    in_w = w_in.shape[2]
    lru_w = conv_w.shape[2]
    kvw = (in_w - 2 * lru_w - 2 * D) // (GQA_GROUP + 2)
    qw = GQA_GROUP * kvw
    q_off = 2 * lru_w
    ga_off = q_off + qw + 2 * kvw
    alpha = (2.0 * depth) ** 0.25

    n_rows = -(-(B + 1) // V7X_SUBLANES) * V7X_SUBLANES
    cv = jnp.zeros((n_rows, D), F32).at[:B].set(c).at[B].set(c_ctx)
    mods = ada_modulation(cv, w_ada_dn, w_ada_up, b_ada).reshape(depth, n_rows, 6, D)
    mod_l = [mods[l, :B] for l in range(depth)]
    mod_c = [mods[l, B:B + 1] for l in range(depth)]

    cos2, sin2 = _rope_tables(T)
    w_in_b = w_in.astype(BF16)
    w_branch_b = w_branch.astype(BF16)
    w_out_b = w_out.astype(BF16)
    w_rg_b = w_rg.astype(BF16)
    w_ig_b = w_ig.astype(BF16)
    w_gu_b = w_gate_up.astype(BF16)
    w_dn_b = w_down.astype(BF16)
    w_router_t = jnp.swapaxes(w_router, 1, 2)

    xl, xc = x, ctx
    hl = modulate_rows(xl, mod_l[0])
    hc = modulate_rows(xc, mod_c[0])
    for l in range(depth):
        last = l == depth - 1
        proj_l = matmul(hl.reshape(B * T, D), w_in_b, l).reshape(B, T, in_w)
        proj_c = matmul(hc.reshape(B * Tc, D), w_in_b, l).reshape(B, Tc, in_w)
        q_l, k_l, v_l = norm_rope(proj_l, q_off, qw, kvw, q_norm_g[l], k_norm_g[l], cos2, sin2)
        q_c, k_c, v_c = norm_rope(proj_c, q_off, qw, kvw, q_norm_g[l], k_norm_g[l], None, None)
        k_all = jnp.concatenate([k_c, k_l], axis=1)
        v_all = jnp.concatenate([v_c, v_l], axis=1)
        att_l = attention(q_l, k_all, v_all)
        lru_l, lru_c = lru_branch(proj_l, proj_c, lru_w, l, conv_w, conv_b, w_rg_b, b_rg, w_ig_b, b_ig, lru_lambda)

        def mixer_tail(xs, hs_proj, att, lru, mod, Ts):
            m = merge_branches(att.reshape(B * Ts, qw), lru.reshape(B * Ts, lru_w), hs_proj.reshape(B * Ts, in_w),
                               ga_off, D, w_branch_b, l)
            mix = matmul(m, w_out_b, l).reshape(B, Ts, D)
            return post_ln(xs, mix, mod, mod, ln_g[l, 0], ln_b[l, 0], alpha, 2, 3, 4, w_router_t[l])

        xl1, hl2, logits_l = mixer_tail(xl, proj_l, att_l, lru_l, mod_l[l], T)
        if last:
            (moe_l,) = expert_choice_ffn([(hl2, logits_l)], w_gu_b, w_dn_b, l)
            (xl,) = post_ln(xl1, moe_l, mod_l[l], mod_l[l], ln_g[l, 1], ln_b[l, 1], alpha, 5, 0, 1, emit_h=False)
        else:
            att_c = attention(q_c, k_c, v_c)
            xc1, hc2, logits_c = mixer_tail(xc, proj_c, att_c, lru_c, mod_c[l], Tc)
            moe_l, moe_c = expert_choice_ffn([(hl2, logits_l), (hc2, logits_c)], w_gu_b, w_dn_b, l)
            xl, hl = post_ln(xl1, moe_l, mod_l[l], mod_l[l + 1], ln_g[l, 1], ln_b[l, 1], alpha, 5, 0, 1)
            xc, hc = post_ln(xc1, moe_c, mod_c[l], mod_c[l + 1], ln_g[l, 1], ln_b[l, 1], alpha, 5, 0, 1)
    return xl
```

```python
import functools
import math

import jax
import jax.numpy as jnp
from jax import lax
from jax.experimental import pallas as pl
from jax.experimental.pallas import tpu as pltpu

HEAD_DIM = 128
GQA_GROUP = 4
GRID_W = 64
ROPE_THETA = 10000.0
LRU_BLOCK = 128
LRU_C = 8.0
CONV_W = 4
CONV_LEFT = 2
N_EXPERTS = 16
CAP_FACTOR = 2
NORM_EPS = 1e-6
LOG2_E = 1.4426950408889634
GELU_K0 = 0.7978845608028654
GELU_K1 = 0.044715
TINY_F32 = 1e-30
LRU_SCAN_UNROLL = 4

V7X_LANES = 128
V7X_SUBLANES = 8
V7X_VMEM_LIMIT_BYTES = 56 * 1024 * 1024
V7X_MXU_DEPTH = 256
FFN_GATE_ROWS = 64

ATTN_TQ = 512
ATTN_TK = 1024
ATTN_SUB = 128
TOKEN_WINDOW = 256
SLOT_SPAN = 64
BF16_ROWS = 16

F32 = jnp.float32
BF16 = jnp.bfloat16
HIGHEST = lax.Precision.HIGHEST


def _tile(n, pref):
    t = min(pref, n)
    while n % t:
        t //= 2
    return t


def _params(*sem):
    return pltpu.CompilerParams(dimension_semantics=sem, vmem_limit_bytes=V7X_VMEM_LIMIT_BYTES)


def _sigmoid(x):
    return 1.0 / (1.0 + jnp.exp(-x))


def _dot(a, b):
    return jnp.dot(a, b, preferred_element_type=F32)


def _ada_kernel(c_ref, wdn_ref, wup_ref, b_ref, o_ref):
    c = c_ref[...]
    s = c * _sigmoid(c)
    t = jnp.dot(s, wdn_ref[0], preferred_element_type=F32, precision=HIGHEST)
    o_ref[0] = jnp.dot(t, wup_ref[0], preferred_element_type=F32, precision=HIGHEST) + b_ref[0]


def ada_modulation(cv, w_dn, w_up, b_up):
    L, D, R = w_dn.shape
    N = w_up.shape[2]
    rows = cv.shape[0]
    tn = _tile(N, 4096)
    return pl.pallas_call(
        _ada_kernel,
        grid=(L, N // tn),
        in_specs=[
            pl.BlockSpec((rows, D), lambda l, j: (0, 0)),
            pl.BlockSpec((1, D, R), lambda l, j: (l, 0, 0)),
            pl.BlockSpec((1, R, tn), lambda l, j: (l, 0, j)),
            pl.BlockSpec((1, 1, tn), lambda l, j: (l, 0, j)),
        ],
        out_specs=pl.BlockSpec((1, rows, tn), lambda l, j: (l, 0, j)),
        out_shape=jax.ShapeDtypeStruct((L, rows, N), F32),
        compiler_params=_params("arbitrary", "arbitrary"),
        name="ada_modulation",
    )(cv, w_dn, w_up, b_up.reshape(L, 1, N))


def _modulate_kernel(x_ref, mod_ref, h_ref):
    sh = mod_ref[0, 0:1, :]
    sc = mod_ref[0, 1:2, :]
    h_ref[0] = (x_ref[0] * (1.0 + sc) + sh).astype(BF16)


def modulate_rows(x, mod):
    B, T, D = x.shape
    tr = _tile(T, 512)
    per_batch = mod.shape[0] > 1
    return pl.pallas_call(
        _modulate_kernel,
        grid=(B, T // tr),
        in_specs=[
            pl.BlockSpec((1, tr, D), lambda b, i: (b, i, 0)),
            pl.BlockSpec((1, 6, D), (lambda b, i: (b, 0, 0)) if per_batch else (lambda b, i: (0, 0, 0))),
        ],
        out_specs=pl.BlockSpec((1, tr, D), lambda b, i: (b, i, 0)),
        out_shape=jax.ShapeDtypeStruct((B, T, D), BF16),
        compiler_params=_params("arbitrary", "arbitrary"),
        name="modulate",
    )(x, mod)


def _matmul_kernel(a_ref, w_ref, o_ref):
    o_ref[...] = _dot(a_ref[...], w_ref[0]).astype(o_ref.dtype)


def matmul(a, w, l, out_dtype=BF16):
    M, K = a.shape
    N = w.shape[2]
    tm, tn = _tile(M, 1024), _tile(N, 1024)
    return pl.pallas_call(
        _matmul_kernel,
        grid=(M // tm, N // tn),
        in_specs=[
            pl.BlockSpec((tm, K), lambda i, j: (i, 0)),
            pl.BlockSpec((1, K, tn), lambda i, j: (l, 0, j)),
        ],
        out_specs=pl.BlockSpec((tm, tn), lambda i, j: (i, j)),
        out_shape=jax.ShapeDtypeStruct((M, N), out_dtype),
        compiler_params=_params("arbitrary", "arbitrary"),
        name="matmul",
    )(a, w)


def _normrope_kernel(*refs, nq, nkv, rope, scale):
    if rope:
        q_ref, kv_ref, qg_ref, kg_ref, cos_ref, sin_ref, qo_ref, ko_ref, vo_ref = refs
        cs = cos_ref[...]
        sn = sin_ref[...]
        lane = lax.broadcasted_iota(jnp.int32, cs.shape, 1)
        even = (lane & 1) == 0
    else:
        q_ref, kv_ref, qg_ref, kg_ref, qo_ref, ko_ref, vo_ref = refs

    def head(xh, g):
        xf = xh.astype(F32)
        y = xf * lax.rsqrt(jnp.mean(xf * xf, axis=-1, keepdims=True) + NORM_EPS) * g
        if rope:
            partner = jnp.where(even, pltpu.roll(y, HEAD_DIM - 1, 1), pltpu.roll(y, 1, 1))
            y = y * cs + partner * sn
        return y

    qg = qg_ref[...]
    kg = kg_ref[...]
    for h in range(nq):
        sl = slice(h * HEAD_DIM, (h + 1) * HEAD_DIM)
        qo_ref[0, :, sl] = (head(q_ref[0, :, sl], qg) * scale).astype(BF16)
    for h in range(nkv):
        sl = slice(h * HEAD_DIM, (h + 1) * HEAD_DIM)
        ko_ref[0, :, sl] = head(kv_ref[0, :, sl], kg).astype(BF16)
    vo_ref[0] = kv_ref[0, :, nkv * HEAD_DIM:]


def norm_rope(proj, q_off, qw, kvw, qg, kg, cos2, sin2):
    B, T, _ = proj.shape
    rope = cos2 is not None
    tr = _tile(T, 512)
    assert q_off % qw == 0 and (q_off + qw) % (2 * kvw) == 0
    qb, kvb = q_off // qw, (q_off + qw) // (2 * kvw)
    in_specs = [
        pl.BlockSpec((1, tr, qw), lambda b, i: (b, i, qb)),
        pl.BlockSpec((1, tr, 2 * kvw), lambda b, i: (b, i, kvb)),
        pl.BlockSpec((1, HEAD_DIM), lambda b, i: (0, 0)),
        pl.BlockSpec((1, HEAD_DIM), lambda b, i: (0, 0)),
    ]
    args = [proj, proj, qg.reshape(1, HEAD_DIM), kg.reshape(1, HEAD_DIM)]
    if rope:
        in_specs += [pl.BlockSpec((tr, HEAD_DIM), lambda b, i: (i, 0))] * 2
        args += [cos2, sin2]
    kern = functools.partial(_normrope_kernel, nq=qw // HEAD_DIM, nkv=kvw // HEAD_DIM, rope=rope,
                             scale=HEAD_DIM ** -0.5)
    return pl.pallas_call(
        kern,
        grid=(B, T // tr),
        in_specs=in_specs,
        out_specs=[
            pl.BlockSpec((1, tr, qw), lambda b, i: (b, i, 0)),
            pl.BlockSpec((1, tr, kvw), lambda b, i: (b, i, 0)),
            pl.BlockSpec((1, tr, kvw), lambda b, i: (b, i, 0)),
        ],
        out_shape=[
            jax.ShapeDtypeStruct((B, T, qw), BF16),
            jax.ShapeDtypeStruct((B, T, kvw), BF16),
            jax.ShapeDtypeStruct((B, T, kvw), BF16),
        ],
        compiler_params=_params("arbitrary", "arbitrary"),
        name="norm_rope",
    )(*args)


def _attn_kernel(q_ref, k_ref, v_ref, o_ref, vx_ref, m_ref, acc_ref, *, tq, first, tk, n_chunks):
    @pl.when(pl.program_id(2) == 0)
    def _():
        vx_ref[:, 0:HEAD_DIM] = v_ref[0]
        vx_ref[:, HEAD_DIM:] = jnp.ones((vx_ref.shape[0], HEAD_DIM), BF16)

    q = jnp.concatenate([q_ref[0, :, g * HEAD_DIM:(g + 1) * HEAD_DIM] for g in range(GQA_GROUP)], axis=0)
    m_ref[...] = jnp.full(m_ref.shape, -jnp.inf, F32)
    acc_ref[...] = jnp.zeros(acc_ref.shape, F32)

    def chunk(off, size):
        k = k_ref[0, pl.ds(off, size), :]
        vx = vx_ref[pl.ds(off, size), :]
        for r in range(0, GQA_GROUP * tq, ATTN_SUB):
            rows = slice(r, r + ATTN_SUB)
            s = lax.dot_general(q[rows], k, (((1,), (1,)), ((), ())), preferred_element_type=F32)
            m_prev = m_ref[rows, :]
            m_next = jnp.maximum(m_prev, jnp.max(s, axis=-1, keepdims=True))
            alpha = jnp.exp(m_prev - m_next)
            p = jnp.exp(s - jnp.concatenate([m_next] * (size // HEAD_DIM), axis=1))
            acc_ref[rows, :] = jnp.concatenate([alpha, alpha], axis=1) * acc_ref[rows, :] + _dot(p.astype(BF16), vx)
            m_ref[rows, :] = m_next

    chunk(0, first)
    for i in range(n_chunks):
        chunk(first + i * tk, tk)
    acc = acc_ref[...]
    o = acc[:, :HEAD_DIM] * (1.0 / acc[:, HEAD_DIM:])
    for g in range(GQA_GROUP):
        o_ref[0, :, g * HEAD_DIM:(g + 1) * HEAD_DIM] = o[g * tq:(g + 1) * tq].astype(BF16)


def attention(q, k, v):
    B, Tq, QW = q.shape
    Tk, KVW = k.shape[1], k.shape[2]
    nkv = KVW // HEAD_DIM
    tq = _tile(Tq, ATTN_TQ)
    tk = min(ATTN_TK, Tk)
    n_chunks = (Tk - 1) // tk
    first = Tk - n_chunks * tk
    assert first % HEAD_DIM == 0 and tk % HEAD_DIM == 0
    gw = GQA_GROUP * HEAD_DIM
    kern = functools.partial(_attn_kernel, tq=tq, first=first, tk=tk, n_chunks=n_chunks)
    return pl.pallas_call(
        kern,
        grid=(B, nkv, Tq // tq),
        in_specs=[
            pl.BlockSpec((1, tq, gw), lambda b, h, i: (b, i, h)),
            pl.BlockSpec((1, Tk, HEAD_DIM), lambda b, h, i: (b, 0, h)),
            pl.BlockSpec((1, Tk, HEAD_DIM), lambda b, h, i: (b, 0, h)),
        ],
        out_specs=pl.BlockSpec((1, tq, gw), lambda b, h, i: (b, i, h)),
        out_shape=jax.ShapeDtypeStruct((B, Tq, QW), BF16),
        scratch_shapes=[
            pltpu.VMEM((Tk, 2 * HEAD_DIM), BF16),
            pltpu.VMEM((GQA_GROUP * tq, HEAD_DIM), F32),
            pltpu.VMEM((GQA_GROUP * tq, 2 * HEAD_DIM), F32),
        ],
        compiler_params=_params("arbitrary", "arbitrary", "arbitrary"),
        name="attention",
    )(q, k, v)


def _gelu_tanh(x):
    half = 0.5 * x
    return half + half * jnp.tanh(x * (GELU_K0 + (GELU_K0 * GELU_K1) * (x * x)))


def _lru_kernel(xl_ref, gl_ref, xc_ref, gc_ref, cw_ref, cb_ref, wr_ref, br_ref, wi_ref, bi_ref, lam_ref,
                yl_ref, yc_ref, pad_ref, al_ref, bl_ref, hl_ref, ac_ref, bc_ref, hc_ref, *, Ll, Lc):
    S = V7X_SUBLANES
    cw = cw_ref[0]
    cb = cb_ref[0]

    def coeffs(x_ref, L, a_ref, b_ref):
        pad_ref[0:S, :] = jnp.zeros((S, LRU_BLOCK), F32)
        pad_ref[S:S + L, :] = x_ref[0].astype(F32)
        pad_ref[S + L:2 * S + L, :] = jnp.zeros((S, LRU_BLOCK), F32)
        u = cb
        for j in range(CONV_W):
            start = S - CONV_LEFT + j
            u = u + pad_ref[start:start + L, :] * cw[j:j + 1, :]
        ub = u.astype(BF16)
        half_u = 0.5 * u
        for d in range(2):
            tr = jnp.tanh(0.5 * (_dot(ub, wr_ref[0, d, 0]) + br_ref[0, d:d + 1, :]))
            ti = jnp.tanh(0.5 * (_dot(ub, wi_ref[0, d, 0]) + bi_ref[0, d:d + 1, :]))
            nl = -lam_ref[0, d:d + 1, :]
            softplus = jnp.maximum(nl, 0.0) + jnp.log(1.0 + jnp.exp(-jnp.abs(nl)))
            k = (-0.5 * LRU_C * LOG2_E) * softplus
            a = jnp.exp2(k + k * tr)
            a_ref[d] = a
            gap = 1.0 - a * a
            b_ref[d] = gap * lax.rsqrt(jnp.maximum(gap, TINY_F32)) * (half_u + half_u * ti)

    row = lax.broadcasted_iota(jnp.int32, (S, LRU_BLOCK), 0)

    def tile_scan(a, b, carry, reverse):
        for d in (1, 2, 4):
            if reverse:
                keep = row < S - d
                shift = S - d
            else:
                keep = row >= d
                shift = d
            b = b + a * jnp.where(keep, pltpu.roll(b, shift, 0), 0.0)
            a = a * jnp.where(keep, pltpu.roll(a, shift, 0), 1.0)
        h = b + a * carry
        last = h[0:1, :] if reverse else h[S - 1:S, :]
        return h, jnp.broadcast_to(last, (S, LRU_BLOCK))

    def scan(L, a_ref, b_ref, h_ref, carry_f, carry_r):
        n = L // S

        def body(s, carry):
            cf, cr = carry
            for j in range(LRU_SCAN_UNROLL):
                rf = pl.ds(pl.multiple_of((s * LRU_SCAN_UNROLL + j) * S, S), S)
                hf, cf = tile_scan(a_ref[0, rf, :], b_ref[0, rf, :], cf, False)
                h_ref[0, rf, :] = hf
                rr = pl.ds(pl.multiple_of((n - 1 - s * LRU_SCAN_UNROLL - j) * S, S), S)
                hr, cr = tile_scan(a_ref[1, rr, :], b_ref[1, rr, :], cr, True)
                h_ref[1, rr, :] = hr
            return cf, cr

        return lax.fori_loop(0, n // LRU_SCAN_UNROLL, body, (carry_f, carry_r))

    def finish(h_ref, g_ref, y_ref):
        y_ref[0] = ((h_ref[0] + h_ref[1]) * _gelu_tanh(g_ref[0].astype(F32))).astype(BF16)

    coeffs(xc_ref, Lc, ac_ref, bc_ref)
    coeffs(xl_ref, Ll, al_ref, bl_ref)
    zero = jnp.zeros((S, LRU_BLOCK), F32)
    end_f, end_r = scan(Lc, ac_ref, bc_ref, hc_ref, zero, zero)
    scan(Ll, al_ref, bl_ref, hl_ref, end_f, end_r)
    finish(hc_ref, gc_ref, yc_ref)
    finish(hl_ref, gl_ref, yl_ref)


def lru_branch(proj_l, proj_c, lru_w, l, conv_w, conv_b, w_rg, b_rg, w_ig, b_ig, lam):
    B, Ll, _ = proj_l.shape
    Lc = proj_c.shape[1]
    nb = lru_w // LRU_BLOCK
    S = V7X_SUBLANES
    assert Ll % (S * LRU_SCAN_UNROLL) == 0 and Lc % (S * LRU_SCAN_UNROLL) == 0
    blk = lambda L, off: pl.BlockSpec((1, L, LRU_BLOCK), lambda b, h: (b, 0, h + off))
    chan = lambda rows: pl.BlockSpec((1, rows, LRU_BLOCK), lambda b, h: (l, 0, h))
    wspec = pl.BlockSpec((1, 2, 1, LRU_BLOCK, LRU_BLOCK), lambda b, h: (l, 0, h, 0, 0))
    kern = functools.partial(_lru_kernel, Ll=Ll, Lc=Lc)
    return pl.pallas_call(
        kern,
        grid=(B, nb),
        in_specs=[blk(Ll, 0), blk(Ll, nb), blk(Lc, 0), blk(Lc, nb), chan(CONV_W), chan(1),
                  wspec, chan(2), wspec, chan(2), chan(2)],
        out_specs=[pl.BlockSpec((1, Ll, LRU_BLOCK), lambda b, h: (b, 0, h)),
                   pl.BlockSpec((1, Lc, LRU_BLOCK), lambda b, h: (b, 0, h))],
        out_shape=[jax.ShapeDtypeStruct((B, Ll, lru_w), BF16), jax.ShapeDtypeStruct((B, Lc, lru_w), BF16)],
        scratch_shapes=[pltpu.VMEM((Ll + 2 * S, LRU_BLOCK), F32)]
        + [pltpu.VMEM((2, Ll, LRU_BLOCK), F32)] * 3 + [pltpu.VMEM((2, Lc, LRU_BLOCK), F32)] * 3,
        compiler_params=_params("arbitrary", "arbitrary"),
        name="rglru",
    )(proj_l, proj_l, proj_c, proj_c, conv_w, conv_b.reshape(conv_b.shape[0], 1, lru_w), w_rg, b_rg, w_ig, b_ig, lam)


def _merge_kernel(att_ref, lru_ref, ga_ref, gl_ref, wa_ref, wl_ref, o_ref):
    ya = _dot(att_ref[...], wa_ref[0, 0])
    yl = _dot(lru_ref[...], wl_ref[0, 0])
    m = _sigmoid(ga_ref[...].astype(F32)) * ya + _sigmoid(gl_ref[...].astype(F32)) * yl
    o_ref[...] = m.astype(BF16)


def merge_branches(att, lru, proj, ga_off, D, w_branch, l):
    M, K = att.shape
    tm = _tile(M, 1024)
    tn = 1024
    while ga_off % tn or D % tn:
        tn //= 2
    ga_b, gl_b = ga_off // tn, (ga_off + D) // tn
    return pl.pallas_call(
        _merge_kernel,
        grid=(M // tm, D // tn),
        in_specs=[
            pl.BlockSpec((tm, K), lambda i, j: (i, 0)),
            pl.BlockSpec((tm, K), lambda i, j: (i, 0)),
            pl.BlockSpec((tm, tn), lambda i, j: (i, ga_b + j)),
            pl.BlockSpec((tm, tn), lambda i, j: (i, gl_b + j)),
            pl.BlockSpec((1, 1, K, tn), lambda i, j: (l, 0, 0, j)),
            pl.BlockSpec((1, 1, K, tn), lambda i, j: (l, 1, 0, j)),
        ],
        out_specs=pl.BlockSpec((tm, tn), lambda i, j: (i, j)),
        out_shape=jax.ShapeDtypeStruct((M, D), BF16),
        compiler_params=_params("arbitrary", "arbitrary"),
        name="merge",
    )(att, lru, proj, proj, w_branch, w_branch)


def _postln_kernel(*refs, gate_row, sh_row, sc_row, alpha, router, emit_h):
    x_ref, y_ref, mod_ref, modn_ref, lng_ref, lnb_ref = refs[:6]
    rest = list(refs[6:])
    wr_ref = rest.pop(0) if router else None
    xo_ref = rest.pop(0)
    ho_ref = rest.pop(0) if emit_h else None
    lg_ref = rest.pop(0) if router else None
    z = alpha * x_ref[0] + mod_ref[0, gate_row:gate_row + 1, :] * y_ref[0].astype(F32)
    mu = jnp.mean(z, axis=-1, keepdims=True)
    zc = z - mu
    var = jnp.mean(zc * zc, axis=-1, keepdims=True)
    xn = zc * lax.rsqrt(var + NORM_EPS) * lng_ref[...] + lnb_ref[...]
    xo_ref[0] = xn
    if emit_h:
        h = xn * (1.0 + modn_ref[0, sc_row:sc_row + 1, :]) + modn_ref[0, sh_row:sh_row + 1, :]
        ho_ref[0] = h.astype(BF16)
        if router:
            lg_ref[0] = lax.dot_general(wr_ref[...], h, (((1,), (1,)), ((), ())),
                                        preferred_element_type=F32, precision=HIGHEST)


def post_ln(x, y, mod, modn, ln_g, ln_b, alpha, gate_row, sh_row, sc_row, w_router_t=None, emit_h=True):
    B, T, D = x.shape
    tr = _tile(T, 256)
    router = w_router_t is not None
    mspec = lambda m: pl.BlockSpec((1, 6, D), (lambda b, i: (b, 0, 0)) if m.shape[0] > 1 else (lambda b, i: (0, 0, 0)))
    row = pl.BlockSpec((1, tr, D), lambda b, i: (b, i, 0))
    vec = pl.BlockSpec((1, D), lambda b, i: (0, 0))
    in_specs = [row, row, mspec(mod), mspec(modn), vec, vec]
    args = [x, y, mod, modn, ln_g.reshape(1, D), ln_b.reshape(1, D)]
    out_specs = [row]
    out_shape = [jax.ShapeDtypeStruct((B, T, D), F32)]
    if router:
        E = w_router_t.shape[0]
        in_specs.append(pl.BlockSpec((E, D), lambda b, i: (0, 0)))
        args.append(w_router_t)
    if emit_h:
        out_specs.append(row)
        out_shape.append(jax.ShapeDtypeStruct((B, T, D), BF16))
    if router:
        out_specs.append(pl.BlockSpec((1, E, tr), lambda b, i: (b, 0, i)))
        out_shape.append(jax.ShapeDtypeStruct((B, E, T), F32))
    kern = functools.partial(_postln_kernel, gate_row=gate_row, sh_row=sh_row, sc_row=sc_row, alpha=alpha,
                             router=router, emit_h=emit_h)
    return pl.pallas_call(
        kern,
        grid=(B, T // tr),
        in_specs=in_specs,
        out_specs=out_specs,
        out_shape=out_shape,
        compiler_params=_params("arbitrary", "arbitrary"),
        name="post_ln",
    )(*args)


def _topk_kernel(lg_ref, pos_ref, gate_ref, cnt_ref, *, T, cap, tw):
    lg = lg_ref[0]
    E = lg.shape[0]
    ex = jnp.exp(lg - jnp.max(lg, axis=0, keepdims=True))
    aff = ex / jnp.sum(ex, axis=0, keepdims=True)
    bits = lax.bitcast_convert_type(aff, jnp.int32)

    def count(mask):
        return jnp.sum(jnp.where(mask, 1.0, 0.0), axis=1, keepdims=True)

    thr = jnp.zeros((E, 1), jnp.int32)
    for bit in range(30, -1, -1):
        cand = thr | (1 << bit)
        thr = jnp.where(count(bits >= cand) >= cap, cand, thr)

    t_row = lax.broadcasted_iota(jnp.int32, (T, V7X_LANES), 0)
    w_col = lax.broadcasted_iota(jnp.int32, (T, V7X_LANES), 1)
    before_window = jnp.where(t_row < w_col * tw, 1.0, 0.0).astype(BF16)
    r_i = lax.broadcasted_iota(jnp.int32, (tw, tw), 0)
    c_i = lax.broadcasted_iota(jnp.int32, (tw, tw), 1)
    strict_upper = jnp.where(r_i < c_i, 1.0, 0.0).astype(BF16)

    def prefix(mask):
        mb = jnp.where(mask, 1.0, 0.0).astype(BF16)
        starts = _dot(mb, before_window)
        parts = [_dot(mb[:, w * tw:(w + 1) * tw], strict_upper) + starts[:, w:w + 1] for w in range(T // tw)]
        return jnp.concatenate(parts, axis=1) if len(parts) > 1 else parts[0], starts

    gt = bits > thr
    eq = bits == thr
    need = cap - count(gt)
    tie_rank, _ = prefix(eq)
    sel = gt | (eq & (tie_rank < need))
    slot, starts = prefix(sel)
    pos_ref[0] = jnp.where(sel, slot.astype(jnp.int32), -1)
    gate_ref[0] = jnp.where(sel, aff, 0.0)
    cnt_ref[0] = starts.astype(jnp.int32)


def expert_topk(logits, cap):
    B, E, T = logits.shape
    tw = min(TOKEN_WINDOW, T)
    kern = functools.partial(_topk_kernel, T=T, cap=cap, tw=tw)
    spec = pl.BlockSpec((1, E, T), lambda b: (b, 0, 0))
    return pl.pallas_call(
        kern,
        grid=(B,),
        in_specs=[spec],
        out_specs=[spec, spec, pl.BlockSpec((1, E, V7X_LANES), lambda b: (b, 0, 0))],
        out_shape=[jax.ShapeDtypeStruct((B, E, T), jnp.int32), jax.ShapeDtypeStruct((B, E, T), F32),
                   jax.ShapeDtypeStruct((B, E, V7X_LANES), jnp.int32)],
        compiler_params=_params("arbitrary"),
        name="expert_topk",
    )(logits)


def _slot_span(cnt_ref, idx, C, sp):
    lo = cnt_ref[idx]
    hi = cnt_ref[idx + 1]
    start = jnp.minimum((lo // BF16_ROWS) * BF16_ROWS, C - sp)
    n_extra = jnp.maximum(hi - start - 1, 0) // sp
    return pl.multiple_of(start, BF16_ROWS), n_extra


def _gather_kernel(cnt_ref, pos_ref, h_ref, xs_ref, *, C, sp, nw, E):
    b = pl.program_id(0)
    w = pl.program_id(2)

    @pl.when(w == 0)
    def _():
        xs_ref[...] = jnp.zeros(xs_ref.shape, BF16)

    hwin = h_ref[0]
    tw = hwin.shape[0]
    slot_i = lax.broadcasted_iota(jnp.int32, (sp, tw), 0)
    spans = [_slot_span(cnt_ref, (b * E + e) * (nw + 1) + w, C, sp) for e in range(E)]
    onehots = [jnp.where(pos_ref[0, e:e + 1, :] == slot_i + spans[e][0], 1.0, 0.0).astype(BF16)
               for e in range(E)]
    rows = _dot(jnp.concatenate(onehots, axis=0), hwin)
    for e in range(E):
        start, n_extra = spans[e]
        xs_ref[0, e, pl.ds(start, sp), :] += rows[e * sp:(e + 1) * sp].astype(BF16)

    @pl.when(sum(n for _, n in spans) > 0)
    def _():
        for e in range(E):
            start, n_extra = spans[e]

            def extra(i, carry):
                j0 = start + sp * (i + 1)
                j0c = pl.multiple_of(jnp.minimum(j0, C - sp), BF16_ROWS)
                slot = slot_i + j0c
                hit = (pos_ref[0, e:e + 1, :] == slot) & (slot >= j0)
                xs_ref[0, e, pl.ds(j0c, sp), :] += _dot(jnp.where(hit, 1.0, 0.0).astype(BF16), h_ref[0]).astype(BF16)
                return carry

            lax.fori_loop(0, n_extra, extra, 0)


def gather_tokens(h, pos, cnt_flat, C):
    B, T, D = h.shape
    E = pos.shape[1]
    tw = min(TOKEN_WINDOW, T)
    nw = T // tw
    sp = min(SLOT_SPAN, C)
    dc = _tile(D, 1024)
    kern = functools.partial(_gather_kernel, C=C, sp=sp, nw=nw, E=E)
    grid_spec = pltpu.PrefetchScalarGridSpec(
        num_scalar_prefetch=1,
        grid=(B, D // dc, nw),
        in_specs=[
            pl.BlockSpec((1, E, tw), lambda b, d, w, cnt: (b, 0, w)),
            pl.BlockSpec((1, tw, dc), lambda b, d, w, cnt: (b, w, d)),
        ],
        out_specs=pl.BlockSpec((1, E, C, dc), lambda b, d, w, cnt: (b, 0, 0, d)),
    )
    return pl.pallas_call(
        kern,
        grid_spec=grid_spec,
        out_shape=jax.ShapeDtypeStruct((B, E, C, D), BF16),
        compiler_params=_params("arbitrary", "arbitrary", "arbitrary"),
        name="moe_gather",
    )(cnt_flat, pos, h)


def _slot_gates(pos_ref, gate_ref, C):
    T = pos_ref.shape[-1]
    rows = min(C, FFN_GATE_ROWS)
    cols = []
    for r0 in range(0, C, rows):
        slot = lax.broadcasted_iota(jnp.int32, (rows, V7X_LANES), 0) + r0
        acc = jnp.zeros((rows, V7X_LANES), F32)
        for t0 in range(0, T, V7X_LANES):
            p = pos_ref[0, 0, :, t0:t0 + V7X_LANES]
            acc = acc + jnp.where(p == slot, gate_ref[0, 0, :, t0:t0 + V7X_LANES], 0.0)
        cols.append(jnp.sum(acc, axis=-1, keepdims=True))
    return cols[0] if len(cols) == 1 else jnp.concatenate(cols, axis=0)


def _ffn_kernel(*refs, F, n):
    ins, (wgu_ref, wd_ref), y_refs = refs[:3 * n], refs[3 * n:3 * n + 2], refs[3 * n + 2:]
    xs_refs, pos_refs, gate_refs = ins[0::3], ins[1::3], ins[2::3]
    x = xs_refs[0][0, 0] if n == 1 else jnp.concatenate([r[0, 0] for r in xs_refs], axis=0)
    gates = [_slot_gates(p, g, r.shape[2]) for r, p, g in zip(xs_refs, pos_refs, gate_refs)]
    gu = _dot(x, wgu_ref[0, 0])
    g = gu[:, :F]
    u = gu[:, F:]
    act = (g * _sigmoid(g) * u).astype(BF16)
    y = _dot(act, wd_ref[0, 0])
    row = 0
    for r, gate in zip(y_refs, gates):
        rows = r.shape[2]
        r[0, 0] = (y[row:row + rows] * gate).astype(BF16)
        row += rows


def expert_ffn(streams, w_gate_up, w_down, l):
    B, E, _, D = streams[0][0].shape
    F = w_down.shape[2]
    slot_spec = lambda xs: pl.BlockSpec((1, 1, xs.shape[2], D), lambda e, b: (b, e, 0, 0))
    tok_spec = lambda T: pl.BlockSpec((1, 1, 1, T), lambda e, b: (b, e, 0, 0))
    in_specs, args = [], []
    for xs, pos, gate in streams:
        T = pos.shape[2]
        in_specs += [slot_spec(xs), tok_spec(T), tok_spec(T)]
        args += [xs, pos.reshape(B, E, 1, T), gate.reshape(B, E, 1, T)]
    kern = functools.partial(_ffn_kernel, F=F, n=len(streams))
    return pl.pallas_call(
        kern,
        grid=(E, B),
        in_specs=in_specs + [
            pl.BlockSpec((1, 1, D, 2 * F), lambda e, b: (l, e, 0, 0)),
            pl.BlockSpec((1, 1, F, D), lambda e, b: (l, e, 0, 0)),
        ],
        out_specs=[slot_spec(xs) for xs, _, _ in streams],
        out_shape=[jax.ShapeDtypeStruct(xs.shape, BF16) for xs, _, _ in streams],
        compiler_params=_params("arbitrary", "arbitrary"),
        name="expert_ffn",
    )(*args, w_gate_up, w_down)


def _scatter_kernel(cnt_ref, post_ref, y_ref, o_ref, acc_ref, *, C, sp, nw, E):
    b = pl.program_id(0)
    w = pl.program_id(2)
    tw = post_ref.shape[2]
    slot_i = lax.broadcasted_iota(jnp.int32, (sp, tw), 0)

    def onehot(e, j0c, j0):
        slot = slot_i + j0c
        hit = post_ref[0, e:e + 1, :] == slot
        if j0 is not None:
            hit = hit & (slot >= j0)
        return jnp.where(hit, 1.0, 0.0).astype(BF16)

    def contribution(experts, starts, j0):
        lhs = [onehot(e, s, j0) for e, s in zip(experts, starts)]
        rhs = [y_ref[0, e, pl.ds(s, sp), :] for e, s in zip(experts, starts)]
        if len(lhs) > 1:
            lhs, rhs = [jnp.concatenate(lhs, axis=0)], [jnp.concatenate(rhs, axis=0)]
        return lax.dot_general(lhs[0], rhs[0], (((0,), (0,)), ((), ())), preferred_element_type=F32)

    spans = [_slot_span(cnt_ref, (b * E + e) * (nw + 1) + w, C, sp) for e in range(E)]
    group = max(1, min(E, V7X_MXU_DEPTH // sp))
    total = None
    for e0 in range(0, E, group):
        es = tuple(range(e0, min(e0 + group, E)))
        r = contribution(es, tuple(spans[e][0] for e in es), None)
        total = r if total is None else total + r
    o_ref[0] = total.astype(o_ref.dtype)

    @pl.when(sum(n for _, n in spans) > 0)
    def _():
        acc_ref[...] = total
        for e in range(E):
            start, n_extra = spans[e]

            def extra(i, carry):
                j0 = start + sp * (i + 1)
                j0c = pl.multiple_of(jnp.minimum(j0, C - sp), BF16_ROWS)
                acc_ref[...] += contribution((e,), (j0c,), j0)
                return carry

            lax.fori_loop(0, n_extra, extra, 0)
        o_ref[0] = acc_ref[...].astype(o_ref.dtype)


def scatter_tokens(y, pos_t, cnt_flat):
    B, E, C, D = y.shape
    T = pos_t.shape[2]
    tw = min(TOKEN_WINDOW, T)
    nw = T // tw
    sp = min(SLOT_SPAN, C)
    dc = _tile(D, 1024)
    kern = functools.partial(_scatter_kernel, C=C, sp=sp, nw=nw, E=E)
    grid_spec = pltpu.PrefetchScalarGridSpec(
        num_scalar_prefetch=1,
        grid=(B, D // dc, nw),
        in_specs=[
            pl.BlockSpec((1, E, tw), lambda b, d, w, cnt: (b, 0, w)),
            pl.BlockSpec((1, E, C, dc), lambda b, d, w, cnt: (b, 0, 0, d)),
        ],
        out_specs=pl.BlockSpec((1, tw, dc), lambda b, d, w, cnt: (b, w, d)),
        scratch_shapes=[pltpu.VMEM((tw, dc), F32)],
    )
    return pl.pallas_call(
        kern,
        grid_spec=grid_spec,
        out_shape=jax.ShapeDtypeStruct((B, T, D), BF16),
        compiler_params=_params("arbitrary", "arbitrary", "arbitrary"),
        name="moe_scatter",
    )(cnt_flat, pos_t, y)


def expert_choice_ffn(streams, w_gate_up, w_down, l):
    routed = []
    for h, logits in streams:
        T, E = h.shape[1], logits.shape[1]
        cap = CAP_FACTOR * T // E
        nw = T // min(TOKEN_WINDOW, T)
        pos, gate, cnt = expert_topk(logits, cap)
        cnt_flat = cnt[:, :, :nw + 1].reshape(-1)
        routed.append((gather_tokens(h, pos, cnt_flat, cap), pos, gate, cnt_flat))
    ys = expert_ffn([(xs, pos, gate) for xs, pos, gate, _ in routed], w_gate_up, w_down, l)
    return [scatter_tokens(y, pos, cnt_flat) for y, (_, pos, _, cnt_flat) in zip(ys, routed)]


def _rope_tables(T):
    rows = T // GRID_W
    row = jnp.repeat(jnp.arange(rows, dtype=F32), GRID_W)
    col = jnp.tile(jnp.arange(GRID_W, dtype=F32), rows)
    n_freq = HEAD_DIM // 4
    inv = ROPE_THETA ** (-jnp.arange(n_freq, dtype=F32) / n_freq)
    ang = jnp.concatenate([row[:, None] * inv, col[:, None] * inv], axis=-1)
    cos2 = jnp.repeat(jnp.cos(ang), 2, axis=-1)
    sin = jnp.sin(ang)
    sin2 = jnp.stack([-sin, sin], axis=-1).reshape(T, HEAD_DIM)
    return cos2, sin2


def kernel(x, c, ctx, c_ctx, w_ada_dn, w_ada_up, b_ada, w_in, q_norm_g, k_norm_g, conv_w, conv_b, w_rg, b_rg, w_ig, b_ig, lru_lambda, w_branch, w_out, ln_g, ln_b, w_router, w_gate_up, w_down):
    B, T, D = x.shape
    Tc = ctx.shape[1]
    depth = w_in.shape[0]
    in_w = w_in.shape[2]
    lru_w = conv_w.shape[2]
    kvw = (in_w - 2 * lru_w - 2 * D) // (GQA_GROUP + 2)
    qw = GQA_GROUP * kvw
    q_off = 2 * lru_w
    ga_off = q_off + qw + 2 * kvw
    alpha = (2.0 * depth) ** 0.25

    n_rows = -(-(B + 1) // V7X_SUBLANES) * V7X_SUBLANES
    cv = jnp.zeros((n_rows, D), F32).at[:B].set(c).at[B].set(c_ctx)
    mods = ada_modulation(cv, w_ada_dn, w_ada_up, b_ada).reshape(depth, n_rows, 6, D)
    mod_l = [mods[l, :B] for l in range(depth)]
    mod_c = [mods[l, B:B + 1] for l in range(depth)]

    cos2, sin2 = _rope_tables(T)
    w_in_b = w_in.astype(BF16)
    w_branch_b = w_branch.astype(BF16)
    w_out_b = w_out.astype(BF16)
    w_rg_b = w_rg.astype(BF16)
    w_ig_b = w_ig.astype(BF16)
    w_gu_b = w_gate_up.astype(BF16)
    w_dn_b = w_down.astype(BF16)
    w_router_t = jnp.swapaxes(w_router, 1, 2)

    xl, xc = x, ctx
    hl = modulate_rows(xl, mod_l[0])
    hc = modulate_rows(xc, mod_c[0])
    for l in range(depth):
        last = l == depth - 1
        proj_l = matmul(hl.reshape(B * T, D), w_in_b, l).reshape(B, T, in_w)
        proj_c = matmul(hc.reshape(B * Tc, D), w_in_b, l).reshape(B, Tc, in_w)
        q_l, k_l, v_l = norm_rope(proj_l, q_off, qw, kvw, q_norm_g[l], k_norm_g[l], cos2, sin2)
        q_c, k_c, v_c = norm_rope(proj_c, q_off, qw, kvw, q_norm_g[l], k_norm_g[l], None, None)
        k_all = jnp.concatenate([k_c, k_l], axis=1)
        v_all = jnp.concatenate([v_c, v_l], axis=1)
        att_l = attention(q_l, k_all, v_all)
        lru_l, lru_c = lru_branch(proj_l, proj_c, lru_w, l, conv_w, conv_b, w_rg_b, b_rg, w_ig_b, b_ig, lru_lambda)

        def mixer_tail(xs, hs_proj, att, lru, mod, Ts):
            m = merge_branches(att.reshape(B * Ts, qw), lru.reshape(B * Ts, lru_w), hs_proj.reshape(B * Ts, in_w),
                               ga_off, D, w_branch_b, l)
            mix = matmul(m, w_out_b, l).reshape(B, Ts, D)
            return post_ln(xs, mix, mod, mod, ln_g[l, 0], ln_b[l, 0], alpha, 2, 3, 4, w_router_t[l])

        xl1, hl2, logits_l = mixer_tail(xl, proj_l, att_l, lru_l, mod_l[l], T)
        if last:
            (moe_l,) = expert_choice_ffn([(hl2, logits_l)], w_gu_b, w_dn_b, l)
            (xl,) = post_ln(xl1, moe_l, mod_l[l], mod_l[l], ln_g[l, 1], ln_b[l, 1], alpha, 5, 0, 1, emit_h=False)
        else:
            att_c = attention(q_c, k_c, v_c)
            xc1, hc2, logits_c = mixer_tail(xc, proj_c, att_c, lru_c, mod_c[l], Tc)
            moe_l, moe_c = expert_choice_ffn([(hl2, logits_l), (hc2, logits_c)], w_gu_b, w_dn_b, l)
            xl, hl = post_ln(xl1, moe_l, mod_l[l], mod_l[l + 1], ln_g[l, 1], ln_b[l, 1], alpha, 5, 0, 1)
            xc, hc = post_ln(xc1, moe_c, mod_c[l], mod_c[l + 1], ln_g[l, 1], ln_b[l, 1], alpha, 5, 0, 1)
    return xl
```

```python
import functools
import math

import jax
import jax.numpy as jnp
from jax import lax
from jax.experimental import pallas as pl
from jax.experimental.pallas import tpu as pltpu

HEAD_DIM = 128
GQA_GROUP = 4
GRID_W = 64
ROPE_THETA = 10000.0
LRU_BLOCK = 128
LRU_C = 8.0
CONV_W = 4
CONV_LEFT = 2
N_EXPERTS = 16
CAP_FACTOR = 2
NORM_EPS = 1e-6
LOG2_E = 1.4426950408889634
GELU_K0 = 0.7978845608028654
GELU_K1 = 0.044715
TINY_F32 = 1e-30
LRU_SCAN_UNROLL = 4

V7X_LANES = 128
V7X_SUBLANES = 8
V7X_VMEM_LIMIT_BYTES = 56 * 1024 * 1024
V7X_MXU_DEPTH = 256
FFN_GATE_ROWS = 64

MATMUL_TILE = 1024
ROW_TILE = 512
LN_ROW_TILE = 256
ADA_TILE = 4096
MOE_COL_TILE = 1024
ATTN_TQ = 512
ATTN_TK = 1024
ATTN_SUB = 128
TOKEN_WINDOW = 256
SLOT_SPAN = 64
BF16_ROWS = 16

F32 = jnp.float32
BF16 = jnp.bfloat16
HIGHEST = lax.Precision.HIGHEST


def _tile(n, pref):
    t = min(pref, n)
    while n % t:
        t //= 2
    return t


def _params(*sem):
    return pltpu.CompilerParams(dimension_semantics=sem, vmem_limit_bytes=V7X_VMEM_LIMIT_BYTES)


def _sigmoid(x):
    return 1.0 / (1.0 + jnp.exp(-x))


def _dot(a, b):
    return jnp.dot(a, b, preferred_element_type=F32)


def _ada_kernel(c_ref, wdn_ref, wup_ref, b_ref, o_ref):
    c = c_ref[...]
    s = c * _sigmoid(c)
    t = jnp.dot(s, wdn_ref[0], preferred_element_type=F32, precision=HIGHEST)
    o_ref[0] = jnp.dot(t, wup_ref[0], preferred_element_type=F32, precision=HIGHEST) + b_ref[0]


def ada_modulation(cv, w_dn, w_up, b_up):
    L, D, R = w_dn.shape
    N = w_up.shape[2]
    rows = cv.shape[0]
    tn = _tile(N, ADA_TILE)
    return pl.pallas_call(
        _ada_kernel,
        grid=(L, N // tn),
        in_specs=[
            pl.BlockSpec((rows, D), lambda l, j: (0, 0)),
            pl.BlockSpec((1, D, R), lambda l, j: (l, 0, 0)),
            pl.BlockSpec((1, R, tn), lambda l, j: (l, 0, j)),
            pl.BlockSpec((1, 1, tn), lambda l, j: (l, 0, j)),
        ],
        out_specs=pl.BlockSpec((1, rows, tn), lambda l, j: (l, 0, j)),
        out_shape=jax.ShapeDtypeStruct((L, rows, N), F32),
        compiler_params=_params("arbitrary", "arbitrary"),
        name="ada_modulation",
    )(cv, w_dn, w_up, b_up.reshape(L, 1, N))


def _modulate_kernel(x_ref, mod_ref, h_ref):
    sh = mod_ref[0, 0:1, :]
    sc = mod_ref[0, 1:2, :]
    h_ref[0] = (x_ref[0] * (1.0 + sc) + sh).astype(BF16)


def modulate_rows(x, mod):
    B, T, D = x.shape
    tr = _tile(T, ROW_TILE)
    per_batch = mod.shape[0] > 1
    return pl.pallas_call(
        _modulate_kernel,
        grid=(B, T // tr),
        in_specs=[
            pl.BlockSpec((1, tr, D), lambda b, i: (b, i, 0)),
            pl.BlockSpec((1, 6, D), (lambda b, i: (b, 0, 0)) if per_batch else (lambda b, i: (0, 0, 0))),
        ],
        out_specs=pl.BlockSpec((1, tr, D), lambda b, i: (b, i, 0)),
        out_shape=jax.ShapeDtypeStruct((B, T, D), BF16),
        compiler_params=_params("arbitrary", "arbitrary"),
        name="modulate",
    )(x, mod)


def _matmul_kernel(a_ref, w_ref, o_ref):
    o_ref[...] = _dot(a_ref[...], w_ref[0]).astype(o_ref.dtype)


def matmul(a, w, l, out_dtype=BF16):
    M, K = a.shape
    N = w.shape[2]
    tm, tn = _tile(M, MATMUL_TILE), _tile(N, MATMUL_TILE)
    return pl.pallas_call(
        _matmul_kernel,
        grid=(M // tm, N // tn),
        in_specs=[
            pl.BlockSpec((tm, K), lambda i, j: (i, 0)),
            pl.BlockSpec((1, K, tn), lambda i, j: (l, 0, j)),
        ],
        out_specs=pl.BlockSpec((tm, tn), lambda i, j: (i, j)),
        out_shape=jax.ShapeDtypeStruct((M, N), out_dtype),
        compiler_params=_params("arbitrary", "arbitrary"),
        name="matmul",
    )(a, w)


def _normrope_kernel(*refs, nq, nkv, rope, scale):
    if rope:
        q_ref, kv_ref, qg_ref, kg_ref, cos_ref, sin_ref, qo_ref, ko_ref, vo_ref = refs
        cs = cos_ref[...]
        sn = sin_ref[...]
        lane = lax.broadcasted_iota(jnp.int32, cs.shape, 1)
        even = (lane & 1) == 0
    else:
        q_ref, kv_ref, qg_ref, kg_ref, qo_ref, ko_ref, vo_ref = refs

    def head(xh, g):
        xf = xh.astype(F32)
        y = xf * lax.rsqrt(jnp.mean(xf * xf, axis=-1, keepdims=True) + NORM_EPS) * g
        if rope:
            partner = jnp.where(even, pltpu.roll(y, HEAD_DIM - 1, 1), pltpu.roll(y, 1, 1))
            y = y * cs + partner * sn
        return y

    qg = qg_ref[...]
    kg = kg_ref[...]
    for h in range(nq):
        sl = slice(h * HEAD_DIM, (h + 1) * HEAD_DIM)
        qo_ref[0, :, sl] = (head(q_ref[0, :, sl], qg) * scale).astype(BF16)
    for h in range(nkv):
        sl = slice(h * HEAD_DIM, (h + 1) * HEAD_DIM)
        ko_ref[0, :, sl] = head(kv_ref[0, :, sl], kg).astype(BF16)
    vo_ref[0] = kv_ref[0, :, nkv * HEAD_DIM:]


def norm_rope(proj, q_off, qw, kvw, qg, kg, cos2, sin2):
    B, T, _ = proj.shape
    rope = cos2 is not None
    tr = _tile(T, ROW_TILE)
    assert q_off % qw == 0 and (q_off + qw) % (2 * kvw) == 0
    qb, kvb = q_off // qw, (q_off + qw) // (2 * kvw)
    in_specs = [
        pl.BlockSpec((1, tr, qw), lambda b, i: (b, i, qb)),
        pl.BlockSpec((1, tr, 2 * kvw), lambda b, i: (b, i, kvb)),
        pl.BlockSpec((1, HEAD_DIM), lambda b, i: (0, 0)),
        pl.BlockSpec((1, HEAD_DIM), lambda b, i: (0, 0)),
    ]
    args = [proj, proj, qg.reshape(1, HEAD_DIM), kg.reshape(1, HEAD_DIM)]
    if rope:
        in_specs += [pl.BlockSpec((tr, HEAD_DIM), lambda b, i: (i, 0))] * 2
        args += [cos2, sin2]
    kern = functools.partial(_normrope_kernel, nq=qw // HEAD_DIM, nkv=kvw // HEAD_DIM, rope=rope,
                             scale=HEAD_DIM ** -0.5)
    return pl.pallas_call(
        kern,
        grid=(B, T // tr),
        in_specs=in_specs,
        out_specs=[
            pl.BlockSpec((1, tr, qw), lambda b, i: (b, i, 0)),
            pl.BlockSpec((1, tr, kvw), lambda b, i: (b, i, 0)),
            pl.BlockSpec((1, tr, kvw), lambda b, i: (b, i, 0)),
        ],
        out_shape=[
            jax.ShapeDtypeStruct((B, T, qw), BF16),
            jax.ShapeDtypeStruct((B, T, kvw), BF16),
            jax.ShapeDtypeStruct((B, T, kvw), BF16),
        ],
        compiler_params=_params("arbitrary", "arbitrary"),
        name="norm_rope",
    )(*args)


def _attn_kernel(*refs, n_seg, tq, first, tk, n_chunks):
    q_ref, kv_refs = refs[0], refs[1:1 + 2 * n_seg]
    o_ref, kx_ref, vx_ref, m_ref, acc_ref = refs[1 + 2 * n_seg:]

    @pl.when(pl.program_id(2) == 0)
    def _():
        off = 0
        for s in range(n_seg):
            n = kv_refs[2 * s].shape[1]
            kx_ref[off:off + n, :] = kv_refs[2 * s][0]
            vx_ref[off:off + n, 0:HEAD_DIM] = kv_refs[2 * s + 1][0]
            off += n
        vx_ref[:, HEAD_DIM:] = jnp.ones((vx_ref.shape[0], HEAD_DIM), BF16)

    q = jnp.concatenate([q_ref[0, :, g * HEAD_DIM:(g + 1) * HEAD_DIM] for g in range(GQA_GROUP)], axis=0)
    m_ref[...] = jnp.full(m_ref.shape, -jnp.inf, F32)
    acc_ref[...] = jnp.zeros(acc_ref.shape, F32)

    def chunk(off, size):
        k = kx_ref[pl.ds(off, size), :]
        vx = vx_ref[pl.ds(off, size), :]
        for r in range(0, GQA_GROUP * tq, ATTN_SUB):
            rows = slice(r, r + ATTN_SUB)
            s = lax.dot_general(q[rows], k, (((1,), (1,)), ((), ())), preferred_element_type=F32)
            m_prev = m_ref[rows, :]
            m_next = jnp.maximum(m_prev, jnp.max(s, axis=-1, keepdims=True))
            alpha = jnp.exp(m_prev - m_next)
            p = jnp.exp(s - jnp.concatenate([m_next] * (size // HEAD_DIM), axis=1))
            acc_ref[rows, :] = jnp.concatenate([alpha, alpha], axis=1) * acc_ref[rows, :] + _dot(p.astype(BF16), vx)
            m_ref[rows, :] = m_next

    chunk(0, first)
    for i in range(n_chunks):
        chunk(first + i * tk, tk)
    acc = acc_ref[...]
    o = acc[:, :HEAD_DIM] * (1.0 / acc[:, HEAD_DIM:])
    for g in range(GQA_GROUP):
        o_ref[0, :, g * HEAD_DIM:(g + 1) * HEAD_DIM] = o[g * tq:(g + 1) * tq].astype(BF16)


def attention(q, kv_segments):
    B, Tq, QW = q.shape
    KVW = kv_segments[0][0].shape[2]
    Tk = sum(k.shape[1] for k, _ in kv_segments)
    nkv = KVW // HEAD_DIM
    tq = _tile(Tq, ATTN_TQ)
    tk = min(ATTN_TK, Tk)
    n_chunks = (Tk - 1) // tk
    first = Tk - n_chunks * tk
    assert first % HEAD_DIM == 0 and tk % HEAD_DIM == 0
    gw = GQA_GROUP * HEAD_DIM
    kern = functools.partial(_attn_kernel, n_seg=len(kv_segments), tq=tq, first=first, tk=tk, n_chunks=n_chunks)
    seg_spec = lambda a: pl.BlockSpec((1, a.shape[1], HEAD_DIM), lambda b, h, i: (b, 0, h))
    return pl.pallas_call(
        kern,
        grid=(B, nkv, Tq // tq),
        in_specs=[pl.BlockSpec((1, tq, gw), lambda b, h, i: (b, i, h))]
        + [seg_spec(a) for kv in kv_segments for a in kv],
        out_specs=pl.BlockSpec((1, tq, gw), lambda b, h, i: (b, i, h)),
        out_shape=jax.ShapeDtypeStruct((B, Tq, QW), BF16),
        scratch_shapes=[
            pltpu.VMEM((Tk, HEAD_DIM), BF16),
            pltpu.VMEM((Tk, 2 * HEAD_DIM), BF16),
            pltpu.VMEM((GQA_GROUP * tq, HEAD_DIM), F32),
            pltpu.VMEM((GQA_GROUP * tq, 2 * HEAD_DIM), F32),
        ],
        compiler_params=_params("arbitrary", "arbitrary", "arbitrary"),
        name="attention",
    )(q, *[a for kv in kv_segments for a in kv])


def _gelu_tanh(x):
    half = 0.5 * x
    return half + half * jnp.tanh(x * (GELU_K0 + (GELU_K0 * GELU_K1) * (x * x)))


def _lru_kernel(xl_ref, gl_ref, xc_ref, gc_ref, cw_ref, cb_ref, wr_ref, br_ref, wi_ref, bi_ref, lam_ref,
                yl_ref, yc_ref, pad_ref, al_ref, bl_ref, hl_ref, ac_ref, bc_ref, hc_ref, *, Ll, Lc):
    S = V7X_SUBLANES
    cw = cw_ref[0]
    cb = cb_ref[0]

    def coeffs(x_ref, L, a_ref, b_ref):
        pad_ref[0:S, :] = jnp.zeros((S, LRU_BLOCK), F32)
        pad_ref[S:S + L, :] = x_ref[0].astype(F32)
        pad_ref[S + L:2 * S + L, :] = jnp.zeros((S, LRU_BLOCK), F32)
        u = cb
        for j in range(CONV_W):
            start = S - CONV_LEFT + j
            u = u + pad_ref[start:start + L, :] * cw[j:j + 1, :]
        ub = u.astype(BF16)
        half_u = 0.5 * u
        for d in range(2):
            tr = jnp.tanh(0.5 * (_dot(ub, wr_ref[0, d, 0]) + br_ref[0, d:d + 1, :]))
            ti = jnp.tanh(0.5 * (_dot(ub, wi_ref[0, d, 0]) + bi_ref[0, d:d + 1, :]))
            nl = -lam_ref[0, d:d + 1, :]
            softplus = jnp.maximum(nl, 0.0) + jnp.log(1.0 + jnp.exp(-jnp.abs(nl)))
            k = (-0.5 * LRU_C * LOG2_E) * softplus
            a = jnp.exp2(k + k * tr)
            a_ref[d] = a
            gap = 1.0 - a * a
            b_ref[d] = gap * lax.rsqrt(jnp.maximum(gap, TINY_F32)) * (half_u + half_u * ti)

    row = lax.broadcasted_iota(jnp.int32, (S, LRU_BLOCK), 0)

    def tile_scan(a, b, carry, reverse):
        for d in (1, 2, 4):
            if reverse:
                keep = row < S - d
                shift = S - d
            else:
                keep = row >= d
                shift = d
            b = b + a * jnp.where(keep, pltpu.roll(b, shift, 0), 0.0)
            a = a * jnp.where(keep, pltpu.roll(a, shift, 0), 1.0)
        h = b + a * carry
        last = h[0:1, :] if reverse else h[S - 1:S, :]
        return h, jnp.broadcast_to(last, (S, LRU_BLOCK))

    def scan(L, a_ref, b_ref, h_ref, carry_f, carry_r):
        n = L // S

        def body(s, carry):
            cf, cr = carry
            for j in range(LRU_SCAN_UNROLL):
                rf = pl.ds(pl.multiple_of((s * LRU_SCAN_UNROLL + j) * S, S), S)
                hf, cf = tile_scan(a_ref[0, rf, :], b_ref[0, rf, :], cf, False)
                h_ref[0, rf, :] = hf
                rr = pl.ds(pl.multiple_of((n - 1 - s * LRU_SCAN_UNROLL - j) * S, S), S)
                hr, cr = tile_scan(a_ref[1, rr, :], b_ref[1, rr, :], cr, True)
                h_ref[1, rr, :] = hr
            return cf, cr

        return lax.fori_loop(0, n // LRU_SCAN_UNROLL, body, (carry_f, carry_r))

    def finish(h_ref, g_ref, y_ref):
        y_ref[0] = ((h_ref[0] + h_ref[1]) * _gelu_tanh(g_ref[0].astype(F32))).astype(BF16)

    coeffs(xc_ref, Lc, ac_ref, bc_ref)
    coeffs(xl_ref, Ll, al_ref, bl_ref)
    zero = jnp.zeros((S, LRU_BLOCK), F32)
    end_f, end_r = scan(Lc, ac_ref, bc_ref, hc_ref, zero, zero)
    scan(Ll, al_ref, bl_ref, hl_ref, end_f, end_r)
    finish(hc_ref, gc_ref, yc_ref)
    finish(hl_ref, gl_ref, yl_ref)


def lru_branch(proj_l, proj_c, lru_w, l, conv_w, conv_b, w_rg, b_rg, w_ig, b_ig, lam):
    B, Ll, _ = proj_l.shape
    Lc = proj_c.shape[1]
    nb = lru_w // LRU_BLOCK
    S = V7X_SUBLANES
    assert Ll % (S * LRU_SCAN_UNROLL) == 0 and Lc % (S * LRU_SCAN_UNROLL) == 0
    blk = lambda L, off: pl.BlockSpec((1, L, LRU_BLOCK), lambda b, h: (b, 0, h + off))
    chan = lambda rows: pl.BlockSpec((1, rows, LRU_BLOCK), lambda b, h: (l, 0, h))
    wspec = pl.BlockSpec((1, 2, 1, LRU_BLOCK, LRU_BLOCK), lambda b, h: (l, 0, h, 0, 0))
    kern = functools.partial(_lru_kernel, Ll=Ll, Lc=Lc)
    return pl.pallas_call(
        kern,
        grid=(B, nb),
        in_specs=[blk(Ll, 0), blk(Ll, nb), blk(Lc, 0), blk(Lc, nb), chan(CONV_W), chan(1),
                  wspec, chan(2), wspec, chan(2), chan(2)],
        out_specs=[pl.BlockSpec((1, Ll, LRU_BLOCK), lambda b, h: (b, 0, h)),
                   pl.BlockSpec((1, Lc, LRU_BLOCK), lambda b, h: (b, 0, h))],
        out_shape=[jax.ShapeDtypeStruct((B, Ll, lru_w), BF16), jax.ShapeDtypeStruct((B, Lc, lru_w), BF16)],
        scratch_shapes=[pltpu.VMEM((Ll + 2 * S, LRU_BLOCK), F32)]
        + [pltpu.VMEM((2, Ll, LRU_BLOCK), F32)] * 3 + [pltpu.VMEM((2, Lc, LRU_BLOCK), F32)] * 3,
        compiler_params=_params("arbitrary", "arbitrary"),
        name="rglru",
    )(proj_l, proj_l, proj_c, proj_c, conv_w, conv_b.reshape(conv_b.shape[0], 1, lru_w), w_rg, b_rg, w_ig, b_ig, lam)


def _merge_kernel(att_ref, lru_ref, ga_ref, gl_ref, wa_ref, wl_ref, o_ref):
    ya = _dot(att_ref[...], wa_ref[0, 0])
    yl = _dot(lru_ref[...], wl_ref[0, 0])
    m = _sigmoid(ga_ref[...].astype(F32)) * ya + _sigmoid(gl_ref[...].astype(F32)) * yl
    o_ref[...] = m.astype(BF16)


def merge_branches(att, lru, proj, ga_off, D, w_branch, l):
    M, K = att.shape
    tm = _tile(M, MATMUL_TILE)
    tn = MATMUL_TILE
    while ga_off % tn or D % tn:
        tn //= 2
    ga_b, gl_b = ga_off // tn, (ga_off + D) // tn
    return pl.pallas_call(
        _merge_kernel,
        grid=(M // tm, D // tn),
        in_specs=[
            pl.BlockSpec((tm, K), lambda i, j: (i, 0)),
            pl.BlockSpec((tm, K), lambda i, j: (i, 0)),
            pl.BlockSpec((tm, tn), lambda i, j: (i, ga_b + j)),
            pl.BlockSpec((tm, tn), lambda i, j: (i, gl_b + j)),
            pl.BlockSpec((1, 1, K, tn), lambda i, j: (l, 0, 0, j)),
            pl.BlockSpec((1, 1, K, tn), lambda i, j: (l, 1, 0, j)),
        ],
        out_specs=pl.BlockSpec((tm, tn), lambda i, j: (i, j)),
        out_shape=jax.ShapeDtypeStruct((M, D), BF16),
        compiler_params=_params("arbitrary", "arbitrary"),
        name="merge",
    )(att, lru, proj, proj, w_branch, w_branch)


def _postln_kernel(*refs, gate_row, sh_row, sc_row, alpha, router, emit_h):
    x_ref, y_ref, mod_ref, modn_ref, lng_ref, lnb_ref = refs[:6]
    rest = list(refs[6:])
    wr_ref = rest.pop(0) if router else None
    xo_ref = rest.pop(0)
    ho_ref = rest.pop(0) if emit_h else None
    lg_ref = rest.pop(0) if router else None
    z = alpha * x_ref[0] + mod_ref[0, gate_row:gate_row + 1, :] * y_ref[0].astype(F32)
    mu = jnp.mean(z, axis=-1, keepdims=True)
    zc = z - mu
    var = jnp.mean(zc * zc, axis=-1, keepdims=True)
    xn = zc * lax.rsqrt(var + NORM_EPS) * lng_ref[...] + lnb_ref[...]
    xo_ref[0] = xn
    if emit_h:
        h = xn * (1.0 + modn_ref[0, sc_row:sc_row + 1, :]) + modn_ref[0, sh_row:sh_row + 1, :]
        ho_ref[0] = h.astype(BF16)
        if router:
            lg_ref[0] = lax.dot_general(wr_ref[...], h, (((1,), (1,)), ((), ())),
                                        preferred_element_type=F32, precision=HIGHEST)


def post_ln(x, y, mod, modn, ln_g, ln_b, alpha, gate_row, sh_row, sc_row, w_router_t=None, emit_h=True):
    B, T, D = x.shape
    tr = _tile(T, LN_ROW_TILE)
    router = w_router_t is not None
    mspec = lambda m: pl.BlockSpec((1, 6, D), (lambda b, i: (b, 0, 0)) if m.shape[0] > 1 else (lambda b, i: (0, 0, 0)))
    row = pl.BlockSpec((1, tr, D), lambda b, i: (b, i, 0))
    vec = pl.BlockSpec((1, D), lambda b, i: (0, 0))
    in_specs = [row, row, mspec(mod), mspec(modn), vec, vec]
    args = [x, y, mod, modn, ln_g.reshape(1, D), ln_b.reshape(1, D)]
    out_specs = [row]
    out_shape = [jax.ShapeDtypeStruct((B, T, D), F32)]
    if router:
        E = w_router_t.shape[0]
        in_specs.append(pl.BlockSpec((E, D), lambda b, i: (0, 0)))
        args.append(w_router_t)
    if emit_h:
        out_specs.append(row)
        out_shape.append(jax.ShapeDtypeStruct((B, T, D), BF16))
    if router:
        out_specs.append(pl.BlockSpec((1, E, tr), lambda b, i: (b, 0, i)))
        out_shape.append(jax.ShapeDtypeStruct((B, E, T), F32))
    kern = functools.partial(_postln_kernel, gate_row=gate_row, sh_row=sh_row, sc_row=sc_row, alpha=alpha,
                             router=router, emit_h=emit_h)
    return pl.pallas_call(
        kern,
        grid=(B, T // tr),
        in_specs=in_specs,
        out_specs=out_specs,
        out_shape=out_shape,
        compiler_params=_params("arbitrary", "arbitrary"),
        name="post_ln",
    )(*args)


def _topk_kernel(lg_ref, pos_ref, gate_ref, cnt_ref, *, T, cap, tw):
    lg = lg_ref[0]
    E = lg.shape[0]
    ex = jnp.exp(lg - jnp.max(lg, axis=0, keepdims=True))
    aff = ex / jnp.sum(ex, axis=0, keepdims=True)
    bits = lax.bitcast_convert_type(aff, jnp.int32)

    def count(mask):
        return jnp.sum(jnp.where(mask, 1.0, 0.0), axis=1, keepdims=True)

    thr = jnp.zeros((E, 1), jnp.int32)
    for bit in range(30, -1, -1):
        cand = thr | (1 << bit)
        thr = jnp.where(count(bits >= cand) >= cap, cand, thr)

    t_row = lax.broadcasted_iota(jnp.int32, (T, V7X_LANES), 0)
    w_col = lax.broadcasted_iota(jnp.int32, (T, V7X_LANES), 1)
    before_window = jnp.where(t_row < w_col * tw, 1.0, 0.0).astype(BF16)
    r_i = lax.broadcasted_iota(jnp.int32, (tw, tw), 0)
    c_i = lax.broadcasted_iota(jnp.int32, (tw, tw), 1)
    strict_upper = jnp.where(r_i < c_i, 1.0, 0.0).astype(BF16)

    def prefix(mask):
        mb = jnp.where(mask, 1.0, 0.0).astype(BF16)
        starts = _dot(mb, before_window)
        parts = [_dot(mb[:, w * tw:(w + 1) * tw], strict_upper) + starts[:, w:w + 1] for w in range(T // tw)]
        return jnp.concatenate(parts, axis=1) if len(parts) > 1 else parts[0], starts

    gt = bits > thr
    eq = bits == thr
    need = cap - count(gt)
    tie_rank, _ = prefix(eq)
    sel = gt | (eq & (tie_rank < need))
    slot, starts = prefix(sel)
    pos_ref[0] = jnp.where(sel, slot.astype(jnp.int32), -1)
    gate_ref[0] = jnp.where(sel, aff, 0.0)
    cnt_ref[0] = starts.astype(jnp.int32)


def expert_topk(logits, cap):
    B, E, T = logits.shape
    tw = min(TOKEN_WINDOW, T)
    kern = functools.partial(_topk_kernel, T=T, cap=cap, tw=tw)
    spec = pl.BlockSpec((1, E, T), lambda b: (b, 0, 0))
    return pl.pallas_call(
        kern,
        grid=(B,),
        in_specs=[spec],
        out_specs=[spec, spec, pl.BlockSpec((1, E, V7X_LANES), lambda b: (b, 0, 0))],
        out_shape=[jax.ShapeDtypeStruct((B, E, T), jnp.int32), jax.ShapeDtypeStruct((B, E, T), F32),
                   jax.ShapeDtypeStruct((B, E, V7X_LANES), jnp.int32)],
        compiler_params=_params("arbitrary"),
        name="expert_topk",
    )(logits)


def _slot_span(cnt_ref, idx, C, sp):
    lo = cnt_ref[idx]
    hi = cnt_ref[idx + 1]
    start = jnp.minimum((lo // BF16_ROWS) * BF16_ROWS, C - sp)
    n_extra = jnp.maximum(hi - start - 1, 0) // sp
    return pl.multiple_of(start, BF16_ROWS), n_extra


def _gather_kernel(cnt_ref, pos_ref, h_ref, xs_ref, *, C, sp, nw, E):
    b = pl.program_id(0)
    w = pl.program_id(2)

    @pl.when(w == 0)
    def _():
        xs_ref[...] = jnp.zeros(xs_ref.shape, BF16)

    hwin = h_ref[0]
    tw = hwin.shape[0]
    slot_i = lax.broadcasted_iota(jnp.int32, (sp, tw), 0)
    spans = [_slot_span(cnt_ref, (b * E + e) * (nw + 1) + w, C, sp) for e in range(E)]
    onehots = [jnp.where(pos_ref[0, e:e + 1, :] == slot_i + spans[e][0], 1.0, 0.0).astype(BF16)
               for e in range(E)]
    rows = _dot(jnp.concatenate(onehots, axis=0), hwin)
    for e in range(E):
        start, n_extra = spans[e]
        xs_ref[0, e, pl.ds(start, sp), :] += rows[e * sp:(e + 1) * sp].astype(BF16)

    @pl.when(sum(n for _, n in spans) > 0)
    def _():
        for e in range(E):
            start, n_extra = spans[e]

            def extra(i, carry):
                j0 = start + sp * (i + 1)
                j0c = pl.multiple_of(jnp.minimum(j0, C - sp), BF16_ROWS)
                slot = slot_i + j0c
                hit = (pos_ref[0, e:e + 1, :] == slot) & (slot >= j0)
                xs_ref[0, e, pl.ds(j0c, sp), :] += _dot(jnp.where(hit, 1.0, 0.0).astype(BF16), h_ref[0]).astype(BF16)
                return carry

            lax.fori_loop(0, n_extra, extra, 0)


def gather_tokens(h, pos, cnt_flat, C):
    B, T, D = h.shape
    E = pos.shape[1]
    tw = min(TOKEN_WINDOW, T)
    nw = T // tw
    sp = min(SLOT_SPAN, C)
    dc = _tile(D, MOE_COL_TILE)
    kern = functools.partial(_gather_kernel, C=C, sp=sp, nw=nw, E=E)
    grid_spec = pltpu.PrefetchScalarGridSpec(
        num_scalar_prefetch=1,
        grid=(B, D // dc, nw),
        in_specs=[
            pl.BlockSpec((1, E, tw), lambda b, d, w, cnt: (b, 0, w)),
            pl.BlockSpec((1, tw, dc), lambda b, d, w, cnt: (b, w, d)),
        ],
        out_specs=pl.BlockSpec((1, E, C, dc), lambda b, d, w, cnt: (b, 0, 0, d)),
    )
    return pl.pallas_call(
        kern,
        grid_spec=grid_spec,
        out_shape=jax.ShapeDtypeStruct((B, E, C, D), BF16),
        compiler_params=_params("arbitrary", "arbitrary", "arbitrary"),
        name="moe_gather",
    )(cnt_flat, pos, h)


def _slot_gates(pos_ref, gate_ref, C):
    T = pos_ref.shape[-1]
    rows = min(C, FFN_GATE_ROWS)
    cols = []
    for r0 in range(0, C, rows):
        slot = lax.broadcasted_iota(jnp.int32, (rows, V7X_LANES), 0) + r0
        acc = jnp.zeros((rows, V7X_LANES), F32)
        for t0 in range(0, T, V7X_LANES):
            p = pos_ref[0, 0, :, t0:t0 + V7X_LANES]
            acc = acc + jnp.where(p == slot, gate_ref[0, 0, :, t0:t0 + V7X_LANES], 0.0)
        cols.append(jnp.sum(acc, axis=-1, keepdims=True))
    return cols[0] if len(cols) == 1 else jnp.concatenate(cols, axis=0)


def _ffn_kernel(*refs, F, n):
    ins, (wgu_ref, wd_ref), y_refs = refs[:3 * n], refs[3 * n:3 * n + 2], refs[3 * n + 2:]
    xs_refs, pos_refs, gate_refs = ins[0::3], ins[1::3], ins[2::3]
    x = xs_refs[0][0, 0] if n == 1 else jnp.concatenate([r[0, 0] for r in xs_refs], axis=0)
    gates = [_slot_gates(p, g, r.shape[2]) for r, p, g in zip(xs_refs, pos_refs, gate_refs)]
    gu = _dot(x, wgu_ref[0, 0])
    g = gu[:, :F]
    u = gu[:, F:]
    act = (g * _sigmoid(g) * u).astype(BF16)
    y = _dot(act, wd_ref[0, 0])
    row = 0
    for r, gate in zip(y_refs, gates):
        rows = r.shape[2]
        r[0, 0] = (y[row:row + rows] * gate).astype(BF16)
        row += rows


def expert_ffn(streams, w_gate_up, w_down, l):
    B, E, _, D = streams[0][0].shape
    F = w_down.shape[2]
    slot_spec = lambda xs: pl.BlockSpec((1, 1, xs.shape[2], D), lambda e, b: (b, e, 0, 0))
    tok_spec = lambda T: pl.BlockSpec((1, 1, 1, T), lambda e, b: (b, e, 0, 0))
    in_specs, args = [], []
    for xs, pos, gate in streams:
        T = pos.shape[2]
        in_specs += [slot_spec(xs), tok_spec(T), tok_spec(T)]
        args += [xs, pos.reshape(B, E, 1, T), gate.reshape(B, E, 1, T)]
    kern = functools.partial(_ffn_kernel, F=F, n=len(streams))
    return pl.pallas_call(
        kern,
        grid=(E, B),
        in_specs=in_specs + [
            pl.BlockSpec((1, 1, D, 2 * F), lambda e, b: (l, e, 0, 0)),
            pl.BlockSpec((1, 1, F, D), lambda e, b: (l, e, 0, 0)),
        ],
        out_specs=[slot_spec(xs) for xs, _, _ in streams],
        out_shape=[jax.ShapeDtypeStruct(xs.shape, BF16) for xs, _, _ in streams],
        compiler_params=_params("arbitrary", "arbitrary"),
        name="expert_ffn",
    )(*args, w_gate_up, w_down)


def _scatter_kernel(cnt_ref, post_ref, y_ref, o_ref, acc_ref, *, C, sp, nw, E):
    b = pl.program_id(0)
    w = pl.program_id(2)
    tw = post_ref.shape[2]
    slot_i = lax.broadcasted_iota(jnp.int32, (sp, tw), 0)

    def onehot(e, j0c, j0):
        slot = slot_i + j0c
        hit = post_ref[0, e:e + 1, :] == slot
        if j0 is not None:
            hit = hit & (slot >= j0)
        return jnp.where(hit, 1.0, 0.0).astype(BF16)

    def contribution(experts, starts, j0):
        lhs = [onehot(e, s, j0) for e, s in zip(experts, starts)]
        rhs = [y_ref[0, e, pl.ds(s, sp), :] for e, s in zip(experts, starts)]
        if len(lhs) > 1:
            lhs, rhs = [jnp.concatenate(lhs, axis=0)], [jnp.concatenate(rhs, axis=0)]
        return lax.dot_general(lhs[0], rhs[0], (((0,), (0,)), ((), ())), preferred_element_type=F32)

    spans = [_slot_span(cnt_ref, (b * E + e) * (nw + 1) + w, C, sp) for e in range(E)]
    group = max(1, min(E, V7X_MXU_DEPTH // sp))
    total = None
    for e0 in range(0, E, group):
        es = tuple(range(e0, min(e0 + group, E)))
        r = contribution(es, tuple(spans[e][0] for e in es), None)
        total = r if total is None else total + r
    o_ref[0] = total.astype(o_ref.dtype)

    @pl.when(sum(n for _, n in spans) > 0)
    def _():
        acc_ref[...] = total
        for e in range(E):
            start, n_extra = spans[e]

            def extra(i, carry):
                j0 = start + sp * (i + 1)
                j0c = pl.multiple_of(jnp.minimum(j0, C - sp), BF16_ROWS)
                acc_ref[...] += contribution((e,), (j0c,), j0)
                return carry

            lax.fori_loop(0, n_extra, extra, 0)
        o_ref[0] = acc_ref[...].astype(o_ref.dtype)


def scatter_tokens(y, pos_t, cnt_flat):
    B, E, C, D = y.shape
    T = pos_t.shape[2]
    tw = min(TOKEN_WINDOW, T)
    nw = T // tw
    sp = min(SLOT_SPAN, C)
    dc = _tile(D, MOE_COL_TILE)
    kern = functools.partial(_scatter_kernel, C=C, sp=sp, nw=nw, E=E)
    grid_spec = pltpu.PrefetchScalarGridSpec(
        num_scalar_prefetch=1,
        grid=(B, D // dc, nw),
        in_specs=[
            pl.BlockSpec((1, E, tw), lambda b, d, w, cnt: (b, 0, w)),
            pl.BlockSpec((1, E, C, dc), lambda b, d, w, cnt: (b, 0, 0, d)),
        ],
        out_specs=pl.BlockSpec((1, tw, dc), lambda b, d, w, cnt: (b, w, d)),
        scratch_shapes=[pltpu.VMEM((tw, dc), F32)],
    )
    return pl.pallas_call(
        kern,
        grid_spec=grid_spec,
        out_shape=jax.ShapeDtypeStruct((B, T, D), BF16),
        compiler_params=_params("arbitrary", "arbitrary", "arbitrary"),
        name="moe_scatter",
    )(cnt_flat, pos_t, y)


def expert_choice_ffn(streams, w_gate_up, w_down, l):
    routed = []
    for h, logits in streams:
        T, E = h.shape[1], logits.shape[1]
        cap = CAP_FACTOR * T // E
        nw = T // min(TOKEN_WINDOW, T)
        pos, gate, cnt = expert_topk(logits, cap)
        cnt_flat = cnt[:, :, :nw + 1].reshape(-1)
        routed.append((gather_tokens(h, pos, cnt_flat, cap), pos, gate, cnt_flat))
    ys = expert_ffn([(xs, pos, gate) for xs, pos, gate, _ in routed], w_gate_up, w_down, l)
    return [scatter_tokens(y, pos, cnt_flat) for y, (_, pos, _, cnt_flat) in zip(ys, routed)]


def _rope_tables(T):
    rows = T // GRID_W
    row = jnp.repeat(jnp.arange(rows, dtype=F32), GRID_W)
    col = jnp.tile(jnp.arange(GRID_W, dtype=F32), rows)
    n_freq = HEAD_DIM // 4
    inv = ROPE_THETA ** (-jnp.arange(n_freq, dtype=F32) / n_freq)
    ang = jnp.concatenate([row[:, None] * inv, col[:, None] * inv], axis=-1)
    cos2 = jnp.repeat(jnp.cos(ang), 2, axis=-1)
    sin = jnp.sin(ang)
    sin2 = jnp.stack([-sin, sin], axis=-1).reshape(T, HEAD_DIM)
    return cos2, sin2


def kernel(x, c, ctx, c_ctx, w_ada_dn, w_ada_up, b_ada, w_in, q_norm_g, k_norm_g, conv_w, conv_b, w_rg, b_rg, w_ig, b_ig, lru_lambda, w_branch, w_out, ln_g, ln_b, w_router, w_gate_up, w_down):
    B, T, D = x.shape
    Tc = ctx.shape[1]
    depth = w_in.shape[0]
    in_w = w_in.shape[2]
    lru_w = conv_w.shape[2]
    kvw = (in_w - 2 * lru_w - 2 * D) // (GQA_GROUP + 2)
    qw = GQA_GROUP * kvw
    q_off = 2 * lru_w
    ga_off = q_off + qw + 2 * kvw
    alpha = (2.0 * depth) ** 0.25

    n_rows = -(-(B + 1) // V7X_SUBLANES) * V7X_SUBLANES
    cv = jnp.zeros((n_rows, D), F32).at[:B].set(c).at[B].set(c_ctx)
    mods = ada_modulation(cv, w_ada_dn, w_ada_up, b_ada).reshape(depth, n_rows, 6, D)
    mod_l = [mods[l, :B] for l in range(depth)]
    mod_c = [mods[l, B:B + 1] for l in range(depth)]

    cos2, sin2 = _rope_tables(T)
    w_in_b = w_in.astype(BF16)
    w_branch_b = w_branch.astype(BF16)
    w_out_b = w_out.astype(BF16)
    w_rg_b = w_rg.astype(BF16)
    w_ig_b = w_ig.astype(BF16)
    w_gu_b = w_gate_up.astype(BF16)
    w_dn_b = w_down.astype(BF16)
    w_router_t = jnp.swapaxes(w_router, 1, 2)

    xl, xc = x, ctx
    hl = modulate_rows(xl, mod_l[0])
    hc = modulate_rows(xc, mod_c[0])
    for l in range(depth):
        last = l == depth - 1
        proj_l = matmul(hl.reshape(B * T, D), w_in_b, l).reshape(B, T, in_w)
        proj_c = matmul(hc.reshape(B * Tc, D), w_in_b, l).reshape(B, Tc, in_w)
        q_l, k_l, v_l = norm_rope(proj_l, q_off, qw, kvw, q_norm_g[l], k_norm_g[l], cos2, sin2)
        q_c, k_c, v_c = norm_rope(proj_c, q_off, qw, kvw, q_norm_g[l], k_norm_g[l], None, None)
        att_l = attention(q_l, [(k_c, v_c), (k_l, v_l)])
        lru_l, lru_c = lru_branch(proj_l, proj_c, lru_w, l, conv_w, conv_b, w_rg_b, b_rg, w_ig_b, b_ig, lru_lambda)

        def mixer_tail(xs, hs_proj, att, lru, mod, Ts):
            m = merge_branches(att.reshape(B * Ts, qw), lru.reshape(B * Ts, lru_w), hs_proj.reshape(B * Ts, in_w),
                               ga_off, D, w_branch_b, l)
            mix = matmul(m, w_out_b, l).reshape(B, Ts, D)
            return post_ln(xs, mix, mod, mod, ln_g[l, 0], ln_b[l, 0], alpha, 2, 3, 4, w_router_t[l])

        xl1, hl2, logits_l = mixer_tail(xl, proj_l, att_l, lru_l, mod_l[l], T)
        if last:
            (moe_l,) = expert_choice_ffn([(hl2, logits_l)], w_gu_b, w_dn_b, l)
            (xl,) = post_ln(xl1, moe_l, mod_l[l], mod_l[l], ln_g[l, 1], ln_b[l, 1], alpha, 5, 0, 1, emit_h=False)
        else:
            att_c = attention(q_c, [(k_c, v_c)])
            xc1, hc2, logits_c = mixer_tail(xc, proj_c, att_c, lru_c, mod_c[l], Tc)
            moe_l, moe_c = expert_choice_ffn([(hl2, logits_l), (hc2, logits_c)], w_gu_b, w_dn_b, l)
            xl, hl = post_ln(xl1, moe_l, mod_l[l], mod_l[l + 1], ln_g[l, 1], ln_b[l, 1], alpha, 5, 0, 1)
            xc, hc = post_ln(xc1, moe_c, mod_c[l], mod_c[l + 1], ln_g[l, 1], ln_b[l, 1], alpha, 5, 0, 1)
    return xl
```

```python
import functools
import math

import jax
import jax.numpy as jnp
from jax import lax
from jax.experimental import pallas as pl
from jax.experimental.pallas import tpu as pltpu

HEAD_DIM = 128
GQA_GROUP = 4
GRID_W = 64
ROPE_THETA = 10000.0
LRU_BLOCK = 128
LRU_C = 8.0
CONV_W = 4
CONV_LEFT = 2
N_EXPERTS = 16
CAP_FACTOR = 2
NORM_EPS = 1e-6
LOG2_E = 1.4426950408889634
GELU_K0 = 0.7978845608028654
GELU_K1 = 0.044715
TINY_F32 = 1e-30
LRU_SCAN_UNROLL = 4

V7X_LANES = 128
V7X_SUBLANES = 8
V7X_VMEM_LIMIT_BYTES = 56 * 1024 * 1024
V7X_MXU_DEPTH = 256
FFN_GATE_ROWS = 64

MATMUL_TILE = 1024
ROW_TILE = 512
LN_ROW_TILE = 256
ADA_TILE = 4096
MOE_COL_TILE = 1024
ATTN_TQ = 512
ATTN_TK = 1024
ATTN_SUB = 128
TOKEN_WINDOW = 256
SLOT_SPAN = 64
BF16_ROWS = 16

F32 = jnp.float32
BF16 = jnp.bfloat16
HIGHEST = lax.Precision.HIGHEST


def _tile(n, pref):
    t = min(pref, n)
    while n % t:
        t //= 2
    return t


def _params(*sem):
    return pltpu.CompilerParams(dimension_semantics=sem, vmem_limit_bytes=V7X_VMEM_LIMIT_BYTES)


def _sigmoid(x):
    return 1.0 / (1.0 + jnp.exp(-x))


def _dot(a, b):
    return jnp.dot(a, b, preferred_element_type=F32)


def _ada_kernel(c_ref, wdn_ref, wup_ref, b_ref, o_ref):
    c = c_ref[...]
    s = c * _sigmoid(c)
    t = jnp.dot(s, wdn_ref[0], preferred_element_type=F32, precision=HIGHEST)
    o_ref[0] = jnp.dot(t, wup_ref[0], preferred_element_type=F32, precision=HIGHEST) + b_ref[0]


def ada_modulation(cv, w_dn, w_up, b_up):
    L, D, R = w_dn.shape
    N = w_up.shape[2]
    rows = cv.shape[0]
    tn = _tile(N, ADA_TILE)
    return pl.pallas_call(
        _ada_kernel,
        grid=(L, N // tn),
        in_specs=[
            pl.BlockSpec((rows, D), lambda l, j: (0, 0)),
            pl.BlockSpec((1, D, R), lambda l, j: (l, 0, 0)),
            pl.BlockSpec((1, R, tn), lambda l, j: (l, 0, j)),
            pl.BlockSpec((1, 1, tn), lambda l, j: (l, 0, j)),
        ],
        out_specs=pl.BlockSpec((1, rows, tn), lambda l, j: (l, 0, j)),
        out_shape=jax.ShapeDtypeStruct((L, rows, N), F32),
        compiler_params=_params("arbitrary", "arbitrary"),
        name="ada_modulation",
    )(cv, w_dn, w_up, b_up.reshape(L, 1, N))


def _modulate_kernel(x_ref, mod_ref, h_ref):
    sh = mod_ref[0, 0:1, :]
    sc = mod_ref[0, 1:2, :]
    h_ref[0] = (x_ref[0] * (1.0 + sc) + sh).astype(BF16)


def modulate_rows(x, mod):
    B, T, D = x.shape
    tr = _tile(T, ROW_TILE)
    per_batch = mod.shape[0] > 1
    return pl.pallas_call(
        _modulate_kernel,
        grid=(B, T // tr),
        in_specs=[
            pl.BlockSpec((1, tr, D), lambda b, i: (b, i, 0)),
            pl.BlockSpec((1, 6, D), (lambda b, i: (b, 0, 0)) if per_batch else (lambda b, i: (0, 0, 0))),
        ],
        out_specs=pl.BlockSpec((1, tr, D), lambda b, i: (b, i, 0)),
        out_shape=jax.ShapeDtypeStruct((B, T, D), BF16),
        compiler_params=_params("arbitrary", "arbitrary"),
        name="modulate",
    )(x, mod)


def _matmul_kernel(a_ref, w_ref, o_ref):
    o_ref[...] = _dot(a_ref[...], w_ref[0]).astype(o_ref.dtype)


def matmul(a, w, l, out_dtype=BF16):
    M, K = a.shape
    N = w.shape[2]
    tm, tn = _tile(M, MATMUL_TILE), _tile(N, MATMUL_TILE)
    return pl.pallas_call(
        _matmul_kernel,
        grid=(M // tm, N // tn),
        in_specs=[
            pl.BlockSpec((tm, K), lambda i, j: (i, 0)),
            pl.BlockSpec((1, K, tn), lambda i, j: (l, 0, j)),
        ],
        out_specs=pl.BlockSpec((tm, tn), lambda i, j: (i, j)),
        out_shape=jax.ShapeDtypeStruct((M, N), out_dtype),
        compiler_params=_params("arbitrary", "arbitrary"),
        name="matmul",
    )(a, w)


def _normrope_kernel(*refs, nq, nkv, rope, scale):
    if rope:
        q_ref, kv_ref, qg_ref, kg_ref, cos_ref, sin_ref, qo_ref, ko_ref, vo_ref = refs
        cs = cos_ref[...]
        sn = sin_ref[...]
        lane = lax.broadcasted_iota(jnp.int32, cs.shape, 1)
        even = (lane & 1) == 0
    else:
        q_ref, kv_ref, qg_ref, kg_ref, qo_ref, ko_ref, vo_ref = refs

    ones = jnp.ones((HEAD_DIM, HEAD_DIM), BF16)

    def head(xh, g):
        xf = xh.astype(F32)
        sq = xf * xf
        sq_hi = sq.astype(BF16)
        sq_lo = (sq - sq_hi.astype(F32)).astype(BF16)
        ms = (_dot(sq_hi, ones) + _dot(sq_lo, ones)) * (1.0 / HEAD_DIM)
        y = xf * lax.rsqrt(ms + NORM_EPS) * g
        if rope:
            partner = jnp.where(even, pltpu.roll(y, HEAD_DIM - 1, 1), pltpu.roll(y, 1, 1))
            y = y * cs + partner * sn
        return y

    qg = qg_ref[...]
    kg = kg_ref[...]
    for h in range(nq):
        sl = slice(h * HEAD_DIM, (h + 1) * HEAD_DIM)
        qo_ref[0, :, sl] = (head(q_ref[0, :, sl], qg) * scale).astype(BF16)
    for h in range(nkv):
        sl = slice(h * HEAD_DIM, (h + 1) * HEAD_DIM)
        ko_ref[0, :, sl] = head(kv_ref[0, :, sl], kg).astype(BF16)
    vo_ref[0] = kv_ref[0, :, nkv * HEAD_DIM:]


def norm_rope(proj, q_off, qw, kvw, qg, kg, cos2, sin2):
    B, T, _ = proj.shape
    rope = cos2 is not None
    tr = _tile(T, ROW_TILE)
    assert q_off % qw == 0 and (q_off + qw) % (2 * kvw) == 0
    qb, kvb = q_off // qw, (q_off + qw) // (2 * kvw)
    in_specs = [
        pl.BlockSpec((1, tr, qw), lambda b, i: (b, i, qb)),
        pl.BlockSpec((1, tr, 2 * kvw), lambda b, i: (b, i, kvb)),
        pl.BlockSpec((1, HEAD_DIM), lambda b, i: (0, 0)),
        pl.BlockSpec((1, HEAD_DIM), lambda b, i: (0, 0)),
    ]
    args = [proj, proj, qg.reshape(1, HEAD_DIM), kg.reshape(1, HEAD_DIM)]
    if rope:
        in_specs += [pl.BlockSpec((tr, HEAD_DIM), lambda b, i: (i, 0))] * 2
        args += [cos2, sin2]
    kern = functools.partial(_normrope_kernel, nq=qw // HEAD_DIM, nkv=kvw // HEAD_DIM, rope=rope,
                             scale=HEAD_DIM ** -0.5)
    return pl.pallas_call(
        kern,
        grid=(B, T // tr),
        in_specs=in_specs,
        out_specs=[
            pl.BlockSpec((1, tr, qw), lambda b, i: (b, i, 0)),
            pl.BlockSpec((1, tr, kvw), lambda b, i: (b, i, 0)),
            pl.BlockSpec((1, tr, kvw), lambda b, i: (b, i, 0)),
        ],
        out_shape=[
            jax.ShapeDtypeStruct((B, T, qw), BF16),
            jax.ShapeDtypeStruct((B, T, kvw), BF16),
            jax.ShapeDtypeStruct((B, T, kvw), BF16),
        ],
        compiler_params=_params("arbitrary", "arbitrary"),
        name="norm_rope",
    )(*args)


def _attn_kernel(*refs, n_seg, tq, first, tk, n_chunks):
    q_ref, kv_refs = refs[0], refs[1:1 + 2 * n_seg]
    o_ref, kx_ref, vx_ref, m_ref, acc_ref = refs[1 + 2 * n_seg:]

    @pl.when(pl.program_id(2) == 0)
    def _():
        off = 0
        for s in range(n_seg):
            n = kv_refs[2 * s].shape[1]
            kx_ref[off:off + n, :] = kv_refs[2 * s][0]
            vx_ref[off:off + n, 0:HEAD_DIM] = kv_refs[2 * s + 1][0]
            off += n
        vx_ref[:, HEAD_DIM:] = jnp.ones((vx_ref.shape[0], HEAD_DIM), BF16)

    q = jnp.concatenate([q_ref[0, :, g * HEAD_DIM:(g + 1) * HEAD_DIM] for g in range(GQA_GROUP)], axis=0)
    m_ref[...] = jnp.full(m_ref.shape, -jnp.inf, F32)
    acc_ref[...] = jnp.zeros(acc_ref.shape, F32)

    def chunk(off, size):
        k = kx_ref[pl.ds(off, size), :]
        vx = vx_ref[pl.ds(off, size), :]
        for r in range(0, GQA_GROUP * tq, ATTN_SUB):
            rows = slice(r, r + ATTN_SUB)
            s = lax.dot_general(q[rows], k, (((1,), (1,)), ((), ())), preferred_element_type=F32)
            m_prev = m_ref[rows, :]
            m_next = jnp.maximum(m_prev, jnp.max(s, axis=-1, keepdims=True))
            alpha = jnp.exp(m_prev - m_next)
            p = jnp.exp(s - jnp.concatenate([m_next] * (size // HEAD_DIM), axis=1))
            acc_ref[rows, :] = jnp.concatenate([alpha, alpha], axis=1) * acc_ref[rows, :] + _dot(p.astype(BF16), vx)
            m_ref[rows, :] = m_next

    chunk(0, first)
    for i in range(n_chunks):
        chunk(first + i * tk, tk)
    acc = acc_ref[...]
    o = acc[:, :HEAD_DIM] * (1.0 / acc[:, HEAD_DIM:])
    for g in range(GQA_GROUP):
        o_ref[0, :, g * HEAD_DIM:(g + 1) * HEAD_DIM] = o[g * tq:(g + 1) * tq].astype(BF16)


def attention(q, kv_segments):
    B, Tq, QW = q.shape
    KVW = kv_segments[0][0].shape[2]
    Tk = sum(k.shape[1] for k, _ in kv_segments)
    nkv = KVW // HEAD_DIM
    tq = _tile(Tq, ATTN_TQ)
    tk = min(ATTN_TK, Tk)
    n_chunks = (Tk - 1) // tk
    first = Tk - n_chunks * tk
    assert first % HEAD_DIM == 0 and tk % HEAD_DIM == 0
    gw = GQA_GROUP * HEAD_DIM
    kern = functools.partial(_attn_kernel, n_seg=len(kv_segments), tq=tq, first=first, tk=tk, n_chunks=n_chunks)
    seg_spec = lambda a: pl.BlockSpec((1, a.shape[1], HEAD_DIM), lambda b, h, i: (b, 0, h))
    return pl.pallas_call(
        kern,
        grid=(B, nkv, Tq // tq),
        in_specs=[pl.BlockSpec((1, tq, gw), lambda b, h, i: (b, i, h))]
        + [seg_spec(a) for kv in kv_segments for a in kv],
        out_specs=pl.BlockSpec((1, tq, gw), lambda b, h, i: (b, i, h)),
        out_shape=jax.ShapeDtypeStruct((B, Tq, QW), BF16),
        scratch_shapes=[
            pltpu.VMEM((Tk, HEAD_DIM), BF16),
            pltpu.VMEM((Tk, 2 * HEAD_DIM), BF16),
            pltpu.VMEM((GQA_GROUP * tq, HEAD_DIM), F32),
            pltpu.VMEM((GQA_GROUP * tq, 2 * HEAD_DIM), F32),
        ],
        compiler_params=_params("arbitrary", "arbitrary", "arbitrary"),
        name="attention",
    )(q, *[a for kv in kv_segments for a in kv])


def _gelu_tanh(x):
    half = 0.5 * x
    return half + half * jnp.tanh(x * (GELU_K0 + (GELU_K0 * GELU_K1) * (x * x)))


def _lru_kernel(xl_ref, gl_ref, xc_ref, gc_ref, cw_ref, cb_ref, wr_ref, br_ref, wi_ref, bi_ref, lam_ref,
                yl_ref, yc_ref, pad_ref, al_ref, bl_ref, hl_ref, ac_ref, bc_ref, hc_ref, *, Ll, Lc):
    S = V7X_SUBLANES
    cw = cw_ref[0]
    cb = cb_ref[0]

    def coeffs(x_ref, L, a_ref, b_ref):
        pad_ref[0:S, :] = jnp.zeros((S, LRU_BLOCK), F32)
        pad_ref[S:S + L, :] = x_ref[0].astype(F32)
        pad_ref[S + L:2 * S + L, :] = jnp.zeros((S, LRU_BLOCK), F32)
        u = cb
        for j in range(CONV_W):
            start = S - CONV_LEFT + j
            u = u + pad_ref[start:start + L, :] * cw[j:j + 1, :]
        ub = u.astype(BF16)
        half_u = 0.5 * u
        for d in range(2):
            tr = jnp.tanh(0.5 * (_dot(ub, wr_ref[0, d, 0]) + br_ref[0, d:d + 1, :]))
            ti = jnp.tanh(0.5 * (_dot(ub, wi_ref[0, d, 0]) + bi_ref[0, d:d + 1, :]))
            nl = -lam_ref[0, d:d + 1, :]
            softplus = jnp.maximum(nl, 0.0) + jnp.log(1.0 + jnp.exp(-jnp.abs(nl)))
            k = (-0.5 * LRU_C * LOG2_E) * softplus
            a = jnp.exp2(k + k * tr)
            a_ref[d] = a
            gap = 1.0 - a * a
            b_ref[d] = gap * lax.rsqrt(jnp.maximum(gap, TINY_F32)) * (half_u + half_u * ti)

    row = lax.broadcasted_iota(jnp.int32, (S, LRU_BLOCK), 0)

    def tile_scan(a, b, carry, reverse):
        for d in (1, 2, 4):
            if reverse:
                keep = row < S - d
                shift = S - d
            else:
                keep = row >= d
                shift = d
            b = b + a * jnp.where(keep, pltpu.roll(b, shift, 0), 0.0)
            a = a * jnp.where(keep, pltpu.roll(a, shift, 0), 1.0)
        h = b + a * carry
        last = h[0:1, :] if reverse else h[S - 1:S, :]
        return h, jnp.broadcast_to(last, (S, LRU_BLOCK))

    def scan(L, a_ref, b_ref, h_ref, carry_f, carry_r):
        n = L // S

        def body(s, carry):
            cf, cr = carry
            for j in range(LRU_SCAN_UNROLL):
                rf = pl.ds(pl.multiple_of((s * LRU_SCAN_UNROLL + j) * S, S), S)
                hf, cf = tile_scan(a_ref[0, rf, :], b_ref[0, rf, :], cf, False)
                h_ref[0, rf, :] = hf
                rr = pl.ds(pl.multiple_of((n - 1 - s * LRU_SCAN_UNROLL - j) * S, S), S)
                hr, cr = tile_scan(a_ref[1, rr, :], b_ref[1, rr, :], cr, True)
                h_ref[1, rr, :] = hr
            return cf, cr

        return lax.fori_loop(0, n // LRU_SCAN_UNROLL, body, (carry_f, carry_r))

    def finish(h_ref, g_ref, y_ref):
        y_ref[0] = ((h_ref[0] + h_ref[1]) * _gelu_tanh(g_ref[0].astype(F32))).astype(BF16)

    coeffs(xc_ref, Lc, ac_ref, bc_ref)
    coeffs(xl_ref, Ll, al_ref, bl_ref)
    zero = jnp.zeros((S, LRU_BLOCK), F32)
    end_f, end_r = scan(Lc, ac_ref, bc_ref, hc_ref, zero, zero)
    scan(Ll, al_ref, bl_ref, hl_ref, end_f, end_r)
    finish(hc_ref, gc_ref, yc_ref)
    finish(hl_ref, gl_ref, yl_ref)


def lru_branch(proj_l, proj_c, lru_w, l, conv_w, conv_b, w_rg, b_rg, w_ig, b_ig, lam):
    B, Ll, _ = proj_l.shape
    Lc = proj_c.shape[1]
    nb = lru_w // LRU_BLOCK
    S = V7X_SUBLANES
    assert Ll % (S * LRU_SCAN_UNROLL) == 0 and Lc % (S * LRU_SCAN_UNROLL) == 0
    blk = lambda L, off: pl.BlockSpec((1, L, LRU_BLOCK), lambda b, h: (b, 0, h + off))
    chan = lambda rows: pl.BlockSpec((1, rows, LRU_BLOCK), lambda b, h: (l, 0, h))
    wspec = pl.BlockSpec((1, 2, 1, LRU_BLOCK, LRU_BLOCK), lambda b, h: (l, 0, h, 0, 0))
    kern = functools.partial(_lru_kernel, Ll=Ll, Lc=Lc)
    return pl.pallas_call(
        kern,
        grid=(B, nb),
        in_specs=[blk(Ll, 0), blk(Ll, nb), blk(Lc, 0), blk(Lc, nb), chan(CONV_W), chan(1),
                  wspec, chan(2), wspec, chan(2), chan(2)],
        out_specs=[pl.BlockSpec((1, Ll, LRU_BLOCK), lambda b, h: (b, 0, h)),
                   pl.BlockSpec((1, Lc, LRU_BLOCK), lambda b, h: (b, 0, h))],
        out_shape=[jax.ShapeDtypeStruct((B, Ll, lru_w), BF16), jax.ShapeDtypeStruct((B, Lc, lru_w), BF16)],
        scratch_shapes=[pltpu.VMEM((Ll + 2 * S, LRU_BLOCK), F32)]
        + [pltpu.VMEM((2, Ll, LRU_BLOCK), F32)] * 3 + [pltpu.VMEM((2, Lc, LRU_BLOCK), F32)] * 3,
        compiler_params=_params("arbitrary", "arbitrary"),
        name="rglru",
    )(proj_l, proj_l, proj_c, proj_c, conv_w, conv_b.reshape(conv_b.shape[0], 1, lru_w), w_rg, b_rg, w_ig, b_ig, lam)


def _merge_kernel(att_ref, lru_ref, ga_ref, gl_ref, wa_ref, wl_ref, o_ref):
    ya = _dot(att_ref[...], wa_ref[0, 0])
    yl = _dot(lru_ref[...], wl_ref[0, 0])
    m = _sigmoid(ga_ref[...].astype(F32)) * ya + _sigmoid(gl_ref[...].astype(F32)) * yl
    o_ref[...] = m.astype(BF16)


def merge_branches(att, lru, proj, ga_off, D, w_branch, l):
    M, K = att.shape
    tm = _tile(M, MATMUL_TILE)
    tn = MATMUL_TILE
    while ga_off % tn or D % tn:
        tn //= 2
    ga_b, gl_b = ga_off // tn, (ga_off + D) // tn
    return pl.pallas_call(
        _merge_kernel,
        grid=(M // tm, D // tn),
        in_specs=[
            pl.BlockSpec((tm, K), lambda i, j: (i, 0)),
            pl.BlockSpec((tm, K), lambda i, j: (i, 0)),
            pl.BlockSpec((tm, tn), lambda i, j: (i, ga_b + j)),
            pl.BlockSpec((tm, tn), lambda i, j: (i, gl_b + j)),
            pl.BlockSpec((1, 1, K, tn), lambda i, j: (l, 0, 0, j)),
            pl.BlockSpec((1, 1, K, tn), lambda i, j: (l, 1, 0, j)),
        ],
        out_specs=pl.BlockSpec((tm, tn), lambda i, j: (i, j)),
        out_shape=jax.ShapeDtypeStruct((M, D), BF16),
        compiler_params=_params("arbitrary", "arbitrary"),
        name="merge",
    )(att, lru, proj, proj, w_branch, w_branch)


def _postln_kernel(*refs, gate_row, sh_row, sc_row, alpha, router, emit_h):
    x_ref, y_ref, mod_ref, modn_ref, lng_ref, lnb_ref = refs[:6]
    rest = list(refs[6:])
    wr_ref = rest.pop(0) if router else None
    xo_ref = rest.pop(0)
    ho_ref = rest.pop(0) if emit_h else None
    lg_ref = rest.pop(0) if router else None
    z = alpha * x_ref[0] + mod_ref[0, gate_row:gate_row + 1, :] * y_ref[0].astype(F32)
    mu = jnp.mean(z, axis=-1, keepdims=True)
    zc = z - mu
    var = jnp.mean(zc * zc, axis=-1, keepdims=True)
    xn = zc * lax.rsqrt(var + NORM_EPS) * lng_ref[...] + lnb_ref[...]
    xo_ref[0] = xn
    if emit_h:
        h = xn * (1.0 + modn_ref[0, sc_row:sc_row + 1, :]) + modn_ref[0, sh_row:sh_row + 1, :]
        ho_ref[0] = h.astype(BF16)
        if router:
            lg_ref[0] = lax.dot_general(wr_ref[...], h, (((1,), (1,)), ((), ())),
                                        preferred_element_type=F32, precision=HIGHEST)


def post_ln(x, y, mod, modn, ln_g, ln_b, alpha, gate_row, sh_row, sc_row, w_router_t=None, emit_h=True):
    B, T, D = x.shape
    tr = _tile(T, LN_ROW_TILE)
    router = w_router_t is not None
    mspec = lambda m: pl.BlockSpec((1, 6, D), (lambda b, i: (b, 0, 0)) if m.shape[0] > 1 else (lambda b, i: (0, 0, 0)))
    row = pl.BlockSpec((1, tr, D), lambda b, i: (b, i, 0))
    vec = pl.BlockSpec((1, D), lambda b, i: (0, 0))
    in_specs = [row, row, mspec(mod), mspec(modn), vec, vec]
    args = [x, y, mod, modn, ln_g.reshape(1, D), ln_b.reshape(1, D)]
    out_specs = [row]
    out_shape = [jax.ShapeDtypeStruct((B, T, D), F32)]
    if router:
        E = w_router_t.shape[0]
        in_specs.append(pl.BlockSpec((E, D), lambda b, i: (0, 0)))
        args.append(w_router_t)
    if emit_h:
        out_specs.append(row)
        out_shape.append(jax.ShapeDtypeStruct((B, T, D), BF16))
    if router:
        out_specs.append(pl.BlockSpec((1, E, tr), lambda b, i: (b, 0, i)))
        out_shape.append(jax.ShapeDtypeStruct((B, E, T), F32))
    kern = functools.partial(_postln_kernel, gate_row=gate_row, sh_row=sh_row, sc_row=sc_row, alpha=alpha,
                             router=router, emit_h=emit_h)
    return pl.pallas_call(
        kern,
        grid=(B, T // tr),
        in_specs=in_specs,
        out_specs=out_specs,
        out_shape=out_shape,
        compiler_params=_params("arbitrary", "arbitrary"),
        name="post_ln",
    )(*args)


def _topk_kernel(lg_ref, pos_ref, gate_ref, cnt_ref, *, T, cap, tw):
    lg = lg_ref[0]
    E = lg.shape[0]
    ex = jnp.exp(lg - jnp.max(lg, axis=0, keepdims=True))
    aff = ex / jnp.sum(ex, axis=0, keepdims=True)
    bits = lax.bitcast_convert_type(aff, jnp.int32)

    def count(mask):
        return jnp.sum(jnp.where(mask, 1.0, 0.0), axis=1, keepdims=True)

    thr = jnp.zeros((E, 1), jnp.int32)
    for bit in range(30, -1, -1):
        cand = thr | (1 << bit)
        thr = jnp.where(count(bits >= cand) >= cap, cand, thr)

    t_row = lax.broadcasted_iota(jnp.int32, (T, V7X_LANES), 0)
    w_col = lax.broadcasted_iota(jnp.int32, (T, V7X_LANES), 1)
    before_window = jnp.where(t_row < w_col * tw, 1.0, 0.0).astype(BF16)
    r_i = lax.broadcasted_iota(jnp.int32, (tw, tw), 0)
    c_i = lax.broadcasted_iota(jnp.int32, (tw, tw), 1)
    strict_upper = jnp.where(r_i < c_i, 1.0, 0.0).astype(BF16)

    def prefix(mask):
        mb = jnp.where(mask, 1.0, 0.0).astype(BF16)
        starts = _dot(mb, before_window)
        parts = [_dot(mb[:, w * tw:(w + 1) * tw], strict_upper) + starts[:, w:w + 1] for w in range(T // tw)]
        return jnp.concatenate(parts, axis=1) if len(parts) > 1 else parts[0], starts

    gt = bits > thr
    eq = bits == thr
    need = cap - count(gt)
    tie_rank, _ = prefix(eq)
    sel = gt | (eq & (tie_rank < need))
    slot, starts = prefix(sel)
    pos_ref[0] = jnp.where(sel, slot.astype(jnp.int32), -1)
    gate_ref[0] = jnp.where(sel, aff, 0.0)
    cnt_ref[0] = starts.astype(jnp.int32)


def expert_topk(logits, cap):
    B, E, T = logits.shape
    tw = min(TOKEN_WINDOW, T)
    kern = functools.partial(_topk_kernel, T=T, cap=cap, tw=tw)
    spec = pl.BlockSpec((1, E, T), lambda b: (b, 0, 0))
    return pl.pallas_call(
        kern,
        grid=(B,),
        in_specs=[spec],
        out_specs=[spec, spec, pl.BlockSpec((1, E, V7X_LANES), lambda b: (b, 0, 0))],
        out_shape=[jax.ShapeDtypeStruct((B, E, T), jnp.int32), jax.ShapeDtypeStruct((B, E, T), F32),
                   jax.ShapeDtypeStruct((B, E, V7X_LANES), jnp.int32)],
        compiler_params=_params("arbitrary"),
        name="expert_topk",
    )(logits)


def _slot_span(cnt_ref, idx, C, sp):
    lo = cnt_ref[idx]
    hi = cnt_ref[idx + 1]
    start = jnp.minimum((lo // BF16_ROWS) * BF16_ROWS, C - sp)
    n_extra = jnp.maximum(hi - start - 1, 0) // sp
    return pl.multiple_of(start, BF16_ROWS), n_extra


def _gather_kernel(cnt_ref, pos_ref, h_ref, xs_ref, *, C, sp, nw, E):
    b = pl.program_id(0)
    w = pl.program_id(2)

    @pl.when(w == 0)
    def _():
        xs_ref[...] = jnp.zeros(xs_ref.shape, BF16)

    hwin = h_ref[0]
    tw = hwin.shape[0]
    slot_i = lax.broadcasted_iota(jnp.int32, (sp, tw), 0)
    spans = [_slot_span(cnt_ref, (b * E + e) * (nw + 1) + w, C, sp) for e in range(E)]
    onehots = [jnp.where(pos_ref[0, e:e + 1, :] == slot_i + spans[e][0], 1.0, 0.0).astype(BF16)
               for e in range(E)]
    rows = _dot(jnp.concatenate(onehots, axis=0), hwin)
    for e in range(E):
        start, n_extra = spans[e]
        xs_ref[0, e, pl.ds(start, sp), :] += rows[e * sp:(e + 1) * sp].astype(BF16)

    @pl.when(sum(n for _, n in spans) > 0)
    def _():
        for e in range(E):
            start, n_extra = spans[e]

            def extra(i, carry):
                j0 = start + sp * (i + 1)
                j0c = pl.multiple_of(jnp.minimum(j0, C - sp), BF16_ROWS)
                slot = slot_i + j0c
                hit = (pos_ref[0, e:e + 1, :] == slot) & (slot >= j0)
                xs_ref[0, e, pl.ds(j0c, sp), :] += _dot(jnp.where(hit, 1.0, 0.0).astype(BF16), h_ref[0]).astype(BF16)
                return carry

            lax.fori_loop(0, n_extra, extra, 0)


def gather_tokens(h, pos, cnt_flat, C):
    B, T, D = h.shape
    E = pos.shape[1]
    tw = min(TOKEN_WINDOW, T)
    nw = T // tw
    sp = min(SLOT_SPAN, C)
    dc = _tile(D, MOE_COL_TILE)
    kern = functools.partial(_gather_kernel, C=C, sp=sp, nw=nw, E=E)
    grid_spec = pltpu.PrefetchScalarGridSpec(
        num_scalar_prefetch=1,
        grid=(B, D // dc, nw),
        in_specs=[
            pl.BlockSpec((1, E, tw), lambda b, d, w, cnt: (b, 0, w)),
            pl.BlockSpec((1, tw, dc), lambda b, d, w, cnt: (b, w, d)),
        ],
        out_specs=pl.BlockSpec((1, E, C, dc), lambda b, d, w, cnt: (b, 0, 0, d)),
    )
    return pl.pallas_call(
        kern,
        grid_spec=grid_spec,
        out_shape=jax.ShapeDtypeStruct((B, E, C, D), BF16),
        compiler_params=_params("arbitrary", "arbitrary", "arbitrary"),
        name="moe_gather",
    )(cnt_flat, pos, h)


def _slot_gates(pos_ref, gate_ref, C):
    T = pos_ref.shape[-1]
    rows = min(C, FFN_GATE_ROWS)
    cols = []
    for r0 in range(0, C, rows):
        slot = lax.broadcasted_iota(jnp.int32, (rows, V7X_LANES), 0) + r0
        acc = jnp.zeros((rows, V7X_LANES), F32)
        for t0 in range(0, T, V7X_LANES):
            p = pos_ref[0, 0, :, t0:t0 + V7X_LANES]
            acc = acc + jnp.where(p == slot, gate_ref[0, 0, :, t0:t0 + V7X_LANES], 0.0)
        cols.append(jnp.sum(acc, axis=-1, keepdims=True))
    return cols[0] if len(cols) == 1 else jnp.concatenate(cols, axis=0)


def _ffn_kernel(*refs, F, n):
    ins, (wgu_ref, wd_ref), y_refs = refs[:3 * n], refs[3 * n:3 * n + 2], refs[3 * n + 2:]
    xs_refs, pos_refs, gate_refs = ins[0::3], ins[1::3], ins[2::3]
    x = xs_refs[0][0, 0] if n == 1 else jnp.concatenate([r[0, 0] for r in xs_refs], axis=0)
    gates = [_slot_gates(p, g, r.shape[2]) for r, p, g in zip(xs_refs, pos_refs, gate_refs)]
    gu = _dot(x, wgu_ref[0, 0])
    g = gu[:, :F]
    u = gu[:, F:]
    act = (g * _sigmoid(g) * u).astype(BF16)
    y = _dot(act, wd_ref[0, 0])
    row = 0
    for r, gate in zip(y_refs, gates):
        rows = r.shape[2]
        r[0, 0] = (y[row:row + rows] * gate).astype(BF16)
        row += rows


def expert_ffn(streams, w_gate_up, w_down, l):
    B, E, _, D = streams[0][0].shape
    F = w_down.shape[2]
    slot_spec = lambda xs: pl.BlockSpec((1, 1, xs.shape[2], D), lambda e, b: (b, e, 0, 0))
    tok_spec = lambda T: pl.BlockSpec((1, 1, 1, T), lambda e, b: (b, e, 0, 0))
    in_specs, args = [], []
    for xs, pos, gate in streams:
        T = pos.shape[2]
        in_specs += [slot_spec(xs), tok_spec(T), tok_spec(T)]
        args += [xs, pos.reshape(B, E, 1, T), gate.reshape(B, E, 1, T)]
    kern = functools.partial(_ffn_kernel, F=F, n=len(streams))
    return pl.pallas_call(
        kern,
        grid=(E, B),
        in_specs=in_specs + [
            pl.BlockSpec((1, 1, D, 2 * F), lambda e, b: (l, e, 0, 0)),
            pl.BlockSpec((1, 1, F, D), lambda e, b: (l, e, 0, 0)),
        ],
        out_specs=[slot_spec(xs) for xs, _, _ in streams],
        out_shape=[jax.ShapeDtypeStruct(xs.shape, BF16) for xs, _, _ in streams],
        compiler_params=_params("arbitrary", "arbitrary"),
        name="expert_ffn",
    )(*args, w_gate_up, w_down)


def _scatter_kernel(cnt_ref, post_ref, y_ref, o_ref, acc_ref, *, C, sp, nw, E):
    b = pl.program_id(0)
    w = pl.program_id(2)
    tw = post_ref.shape[2]
    slot_i = lax.broadcasted_iota(jnp.int32, (sp, tw), 0)

    def onehot(e, j0c, j0):
        slot = slot_i + j0c
        hit = post_ref[0, e:e + 1, :] == slot
        if j0 is not None:
            hit = hit & (slot >= j0)
        return jnp.where(hit, 1.0, 0.0).astype(BF16)

    def contribution(experts, starts, j0):
        lhs = [onehot(e, s, j0) for e, s in zip(experts, starts)]
        rhs = [y_ref[0, e, pl.ds(s, sp), :] for e, s in zip(experts, starts)]
        if len(lhs) > 1:
            lhs, rhs = [jnp.concatenate(lhs, axis=0)], [jnp.concatenate(rhs, axis=0)]
        return lax.dot_general(lhs[0], rhs[0], (((0,), (0,)), ((), ())), preferred_element_type=F32)

    spans = [_slot_span(cnt_ref, (b * E + e) * (nw + 1) + w, C, sp) for e in range(E)]
    group = max(1, min(E, V7X_MXU_DEPTH // sp))
    total = None
    for e0 in range(0, E, group):
        es = tuple(range(e0, min(e0 + group, E)))
        r = contribution(es, tuple(spans[e][0] for e in es), None)
        total = r if total is None else total + r
    o_ref[0] = total.astype(o_ref.dtype)

    @pl.when(sum(n for _, n in spans) > 0)
    def _():
        acc_ref[...] = total
        for e in range(E):
            start, n_extra = spans[e]

            def extra(i, carry):
                j0 = start + sp * (i + 1)
                j0c = pl.multiple_of(jnp.minimum(j0, C - sp), BF16_ROWS)
                acc_ref[...] += contribution((e,), (j0c,), j0)
                return carry

            lax.fori_loop(0, n_extra, extra, 0)
        o_ref[0] = acc_ref[...].astype(o_ref.dtype)


def scatter_tokens(y, pos_t, cnt_flat):
    B, E, C, D = y.shape
    T = pos_t.shape[2]
    tw = min(TOKEN_WINDOW, T)
    nw = T // tw
    sp = min(SLOT_SPAN, C)
    dc = _tile(D, MOE_COL_TILE)
    kern = functools.partial(_scatter_kernel, C=C, sp=sp, nw=nw, E=E)
    grid_spec = pltpu.PrefetchScalarGridSpec(
        num_scalar_prefetch=1,
        grid=(B, D // dc, nw),
        in_specs=[
            pl.BlockSpec((1, E, tw), lambda b, d, w, cnt: (b, 0, w)),
            pl.BlockSpec((1, E, C, dc), lambda b, d, w, cnt: (b, 0, 0, d)),
        ],
        out_specs=pl.BlockSpec((1, tw, dc), lambda b, d, w, cnt: (b, w, d)),
        scratch_shapes=[pltpu.VMEM((tw, dc), F32)],
    )
    return pl.pallas_call(
        kern,
        grid_spec=grid_spec,
        out_shape=jax.ShapeDtypeStruct((B, T, D), BF16),
        compiler_params=_params("arbitrary", "arbitrary", "arbitrary"),
        name="moe_scatter",
    )(cnt_flat, pos_t, y)


def expert_choice_ffn(streams, w_gate_up, w_down, l):
    routed = []
    for h, logits in streams:
        T, E = h.shape[1], logits.shape[1]
        cap = CAP_FACTOR * T // E
        nw = T // min(TOKEN_WINDOW, T)
        pos, gate, cnt = expert_topk(logits, cap)
        cnt_flat = cnt[:, :, :nw + 1].reshape(-1)
        routed.append((gather_tokens(h, pos, cnt_flat, cap), pos, gate, cnt_flat))
    ys = expert_ffn([(xs, pos, gate) for xs, pos, gate, _ in routed], w_gate_up, w_down, l)
    return [scatter_tokens(y, pos, cnt_flat) for y, (_, pos, _, cnt_flat) in zip(ys, routed)]


def _rope_tables(T):
    rows = T // GRID_W
    row = jnp.repeat(jnp.arange(rows, dtype=F32), GRID_W)
    col = jnp.tile(jnp.arange(GRID_W, dtype=F32), rows)
    n_freq = HEAD_DIM // 4
    inv = ROPE_THETA ** (-jnp.arange(n_freq, dtype=F32) / n_freq)
    ang = jnp.concatenate([row[:, None] * inv, col[:, None] * inv], axis=-1)
    cos2 = jnp.repeat(jnp.cos(ang), 2, axis=-1)
    sin = jnp.sin(ang)
    sin2 = jnp.stack([-sin, sin], axis=-1).reshape(T, HEAD_DIM)
    return cos2, sin2


def kernel(x, c, ctx, c_ctx, w_ada_dn, w_ada_up, b_ada, w_in, q_norm_g, k_norm_g, conv_w, conv_b, w_rg, b_rg, w_ig, b_ig, lru_lambda, w_branch, w_out, ln_g, ln_b, w_router, w_gate_up, w_down):
    B, T, D = x.shape
    Tc = ctx.shape[1]
    depth = w_in.shape[0]
    in_w = w_in.shape[2]
    lru_w = conv_w.shape[2]
    kvw = (in_w - 2 * lru_w - 2 * D) // (GQA_GROUP + 2)
    qw = GQA_GROUP * kvw
    q_off = 2 * lru_w
    ga_off = q_off + qw + 2 * kvw
    alpha = (2.0 * depth) ** 0.25

    n_rows = -(-(B + 1) // V7X_SUBLANES) * V7X_SUBLANES
    cv = jnp.zeros((n_rows, D), F32).at[:B].set(c).at[B].set(c_ctx)
    mods = ada_modulation(cv, w_ada_dn, w_ada_up, b_ada).reshape(depth, n_rows, 6, D)
    mod_l = [mods[l, :B] for l in range(depth)]
    mod_c = [mods[l, B:B + 1] for l in range(depth)]

    cos2, sin2 = _rope_tables(T)
    w_in_b = w_in.astype(BF16)
    w_branch_b = w_branch.astype(BF16)
    w_out_b = w_out.astype(BF16)
    w_rg_b = w_rg.astype(BF16)
    w_ig_b = w_ig.astype(BF16)
    w_gu_b = w_gate_up.astype(BF16)
    w_dn_b = w_down.astype(BF16)
    w_router_t = jnp.swapaxes(w_router, 1, 2)

    xl, xc = x, ctx
    hl = modulate_rows(xl, mod_l[0])
    hc = modulate_rows(xc, mod_c[0])
    for l in range(depth):
        last = l == depth - 1
        proj_l = matmul(hl.reshape(B * T, D), w_in_b, l).reshape(B, T, in_w)
        proj_c = matmul(hc.reshape(B * Tc, D), w_in_b, l).reshape(B, Tc, in_w)
        q_l, k_l, v_l = norm_rope(proj_l, q_off, qw, kvw, q_norm_g[l], k_norm_g[l], cos2, sin2)
        q_c, k_c, v_c = norm_rope(proj_c, q_off, qw, kvw, q_norm_g[l], k_norm_g[l], None, None)
        att_l = attention(q_l, [(k_c, v_c), (k_l, v_l)])
        lru_l, lru_c = lru_branch(proj_l, proj_c, lru_w, l, conv_w, conv_b, w_rg_b, b_rg, w_ig_b, b_ig, lru_lambda)

        def mixer_tail(xs, hs_proj, att, lru, mod, Ts):
            m = merge_branches(att.reshape(B * Ts, qw), lru.reshape(B * Ts, lru_w), hs_proj.reshape(B * Ts, in_w),
                               ga_off, D, w_branch_b, l)
            mix = matmul(m, w_out_b, l).reshape(B, Ts, D)
            return post_ln(xs, mix, mod, mod, ln_g[l, 0], ln_b[l, 0], alpha, 2, 3, 4, w_router_t[l])

        xl1, hl2, logits_l = mixer_tail(xl, proj_l, att_l, lru_l, mod_l[l], T)
        if last:
            (moe_l,) = expert_choice_ffn([(hl2, logits_l)], w_gu_b, w_dn_b, l)
            (xl,) = post_ln(xl1, moe_l, mod_l[l], mod_l[l], ln_g[l, 1], ln_b[l, 1], alpha, 5, 0, 1, emit_h=False)
        else:
            att_c = attention(q_c, [(k_c, v_c)])
            xc1, hc2, logits_c = mixer_tail(xc, proj_c, att_c, lru_c, mod_c[l], Tc)
            moe_l, moe_c = expert_choice_ffn([(hl2, logits_l), (hc2, logits_c)], w_gu_b, w_dn_b, l)
            xl, hl = post_ln(xl1, moe_l, mod_l[l], mod_l[l + 1], ln_g[l, 1], ln_b[l, 1], alpha, 5, 0, 1)
            xc, hc = post_ln(xc1, moe_c, mod_c[l], mod_c[l + 1], ln_g[l, 1], ln_b[l, 1], alpha, 5, 0, 1)
    return xl
```

```python
import functools
import math

import jax
import jax.numpy as jnp
from jax import lax
from jax.experimental import pallas as pl
from jax.experimental.pallas import tpu as pltpu

HEAD_DIM = 128
GQA_GROUP = 4
GRID_W = 64
ROPE_THETA = 10000.0
LRU_BLOCK = 128
LRU_C = 8.0
CONV_W = 4
CONV_LEFT = 2
N_EXPERTS = 16
CAP_FACTOR = 2
NORM_EPS = 1e-6
LOG2_E = 1.4426950408889634
GELU_K0 = 0.7978845608028654
GELU_K1 = 0.044715
TINY_F32 = 1e-30
LRU_SCAN_UNROLL = 4

V7X_LANES = 128
V7X_SUBLANES = 8
V7X_VMEM_LIMIT_BYTES = 56 * 1024 * 1024
V7X_MXU_DEPTH = 256
FFN_GATE_ROWS = 64

MATMUL_TILE = 1024
ROW_TILE = 512
LN_ROW_TILE = 256
ADA_TILE = 4096
MOE_COL_TILE = 1024
ATTN_TQ = 512
ATTN_TK = 1024
ATTN_SUB = 128
TOKEN_WINDOW = 256
SLOT_SPAN = 64
BF16_ROWS = 16

F32 = jnp.float32
BF16 = jnp.bfloat16
HIGHEST = lax.Precision.HIGHEST


def _tile(n, pref):
    t = min(pref, n)
    while n % t:
        t //= 2
    return t


def _params(*sem):
    return pltpu.CompilerParams(dimension_semantics=sem, vmem_limit_bytes=V7X_VMEM_LIMIT_BYTES)


def _sigmoid(x):
    return 1.0 / (1.0 + jnp.exp(-x))


def _dot(a, b):
    return jnp.dot(a, b, preferred_element_type=F32)


def _ada_kernel(c_ref, wdn_ref, wup_ref, b_ref, o_ref):
    c = c_ref[...]
    s = c * _sigmoid(c)
    t = jnp.dot(s, wdn_ref[0], preferred_element_type=F32, precision=HIGHEST)
    o_ref[0] = jnp.dot(t, wup_ref[0], preferred_element_type=F32, precision=HIGHEST) + b_ref[0]


def ada_modulation(cv, w_dn, w_up, b_up):
    L, D, R = w_dn.shape
    N = w_up.shape[2]
    rows = cv.shape[0]
    tn = _tile(N, ADA_TILE)
    return pl.pallas_call(
        _ada_kernel,
        grid=(L, N // tn),
        in_specs=[
            pl.BlockSpec((rows, D), lambda l, j: (0, 0)),
            pl.BlockSpec((1, D, R), lambda l, j: (l, 0, 0)),
            pl.BlockSpec((1, R, tn), lambda l, j: (l, 0, j)),
            pl.BlockSpec((1, 1, tn), lambda l, j: (l, 0, j)),
        ],
        out_specs=pl.BlockSpec((1, rows, tn), lambda l, j: (l, 0, j)),
        out_shape=jax.ShapeDtypeStruct((L, rows, N), F32),
        compiler_params=_params("arbitrary", "arbitrary"),
        name="ada_modulation",
    )(cv, w_dn, w_up, b_up.reshape(L, 1, N))


def _modulate_kernel(x_ref, mod_ref, h_ref):
    sh = mod_ref[0, 0:1, :]
    sc = mod_ref[0, 1:2, :]
    h_ref[0] = (x_ref[0] * (1.0 + sc) + sh).astype(BF16)


def modulate_rows(x, mod):
    B, T, D = x.shape
    tr = _tile(T, ROW_TILE)
    per_batch = mod.shape[0] > 1
    return pl.pallas_call(
        _modulate_kernel,
        grid=(B, T // tr),
        in_specs=[
            pl.BlockSpec((1, tr, D), lambda b, i: (b, i, 0)),
            pl.BlockSpec((1, 6, D), (lambda b, i: (b, 0, 0)) if per_batch else (lambda b, i: (0, 0, 0))),
        ],
        out_specs=pl.BlockSpec((1, tr, D), lambda b, i: (b, i, 0)),
        out_shape=jax.ShapeDtypeStruct((B, T, D), BF16),
        compiler_params=_params("arbitrary", "arbitrary"),
        name="modulate",
    )(x, mod)


def _matmul_kernel(a_ref, w_ref, o_ref):
    o_ref[...] = _dot(a_ref[...], w_ref[0]).astype(o_ref.dtype)


def matmul(a, w, l, out_dtype=BF16):
    M, K = a.shape
    N = w.shape[2]
    tm, tn = _tile(M, MATMUL_TILE), _tile(N, MATMUL_TILE)
    return pl.pallas_call(
        _matmul_kernel,
        grid=(M // tm, N // tn),
        in_specs=[
            pl.BlockSpec((tm, K), lambda i, j: (i, 0)),
            pl.BlockSpec((1, K, tn), lambda i, j: (l, 0, j)),
        ],
        out_specs=pl.BlockSpec((tm, tn), lambda i, j: (i, j)),
        out_shape=jax.ShapeDtypeStruct((M, N), out_dtype),
        compiler_params=_params("arbitrary", "arbitrary"),
        name="matmul",
    )(a, w)


def _normrope_kernel(*refs, nq, nkv, rope, scale):
    if rope:
        q_ref, kv_ref, qg_ref, kg_ref, cos_ref, sin_ref, qo_ref, ko_ref, vo_ref = refs
        cs = cos_ref[...]
        sn = sin_ref[...]
        lane = lax.broadcasted_iota(jnp.int32, cs.shape, 1)
        even = (lane & 1) == 0
    else:
        q_ref, kv_ref, qg_ref, kg_ref, qo_ref, ko_ref, vo_ref = refs

    ones = jnp.ones((HEAD_DIM, HEAD_DIM), BF16)

    def head(xh, g):
        xf = xh.astype(F32)
        sq = xf * xf
        sq_hi = sq.astype(BF16)
        sq_lo = (sq - sq_hi.astype(F32)).astype(BF16)
        ms = (_dot(sq_hi, ones) + _dot(sq_lo, ones)) * (1.0 / HEAD_DIM)
        y = xf * lax.rsqrt(ms + NORM_EPS) * g
        if rope:
            partner = jnp.where(even, pltpu.roll(y, HEAD_DIM - 1, 1), pltpu.roll(y, 1, 1))
            y = y * cs + partner * sn
        return y

    qg = qg_ref[...]
    kg = kg_ref[...]
    for h in range(nq):
        sl = slice(h * HEAD_DIM, (h + 1) * HEAD_DIM)
        qo_ref[0, :, sl] = (head(q_ref[0, :, sl], qg) * scale).astype(BF16)
    for h in range(nkv):
        sl = slice(h * HEAD_DIM, (h + 1) * HEAD_DIM)
        ko_ref[0, :, sl] = head(kv_ref[0, :, sl], kg).astype(BF16)
    vo_ref[0] = kv_ref[0, :, nkv * HEAD_DIM:]


def norm_rope(proj, q_off, qw, kvw, qg, kg, cos2, sin2):
    B, T, _ = proj.shape
    rope = cos2 is not None
    tr = _tile(T, ROW_TILE)
    assert q_off % qw == 0 and (q_off + qw) % (2 * kvw) == 0
    qb, kvb = q_off // qw, (q_off + qw) // (2 * kvw)
    in_specs = [
        pl.BlockSpec((1, tr, qw), lambda b, i: (b, i, qb)),
        pl.BlockSpec((1, tr, 2 * kvw), lambda b, i: (b, i, kvb)),
        pl.BlockSpec((1, HEAD_DIM), lambda b, i: (0, 0)),
        pl.BlockSpec((1, HEAD_DIM), lambda b, i: (0, 0)),
    ]
    args = [proj, proj, qg.reshape(1, HEAD_DIM), kg.reshape(1, HEAD_DIM)]
    if rope:
        in_specs += [pl.BlockSpec((tr, HEAD_DIM), lambda b, i: (i, 0))] * 2
        args += [cos2, sin2]
    kern = functools.partial(_normrope_kernel, nq=qw // HEAD_DIM, nkv=kvw // HEAD_DIM, rope=rope,
                             scale=HEAD_DIM ** -0.5)
    return pl.pallas_call(
        kern,
        grid=(B, T // tr),
        in_specs=in_specs,
        out_specs=[
            pl.BlockSpec((1, tr, qw), lambda b, i: (b, i, 0)),
            pl.BlockSpec((1, tr, kvw), lambda b, i: (b, i, 0)),
            pl.BlockSpec((1, tr, kvw), lambda b, i: (b, i, 0)),
        ],
        out_shape=[
            jax.ShapeDtypeStruct((B, T, qw), BF16),
            jax.ShapeDtypeStruct((B, T, kvw), BF16),
            jax.ShapeDtypeStruct((B, T, kvw), BF16),
        ],
        compiler_params=_params("arbitrary", "arbitrary"),
        name="norm_rope",
    )(*args)


def _attn_kernel(*refs, n_seg, tq, first, tk, n_chunks):
    q_ref, kv_refs = refs[0], refs[1:1 + 2 * n_seg]
    o_ref, kx_ref, vx_ref, m_ref, acc_ref = refs[1 + 2 * n_seg:]

    @pl.when(pl.program_id(2) == 0)
    def _():
        off = 0
        for s in range(n_seg):
            n = kv_refs[2 * s].shape[1]
            kx_ref[off:off + n, :] = kv_refs[2 * s][0]
            vx_ref[off:off + n, 0:HEAD_DIM] = kv_refs[2 * s + 1][0]
            off += n
        vx_ref[:, HEAD_DIM:] = jnp.ones((vx_ref.shape[0], HEAD_DIM), BF16)

    q = jnp.concatenate([q_ref[0, :, g * HEAD_DIM:(g + 1) * HEAD_DIM] for g in range(GQA_GROUP)], axis=0)
    m_ref[...] = jnp.full(m_ref.shape, -jnp.inf, F32)
    acc_ref[...] = jnp.zeros(acc_ref.shape, F32)

    def chunk(off, size):
        k = kx_ref[pl.ds(off, size), :]
        vx = vx_ref[pl.ds(off, size), :]
        for r in range(0, GQA_GROUP * tq, ATTN_SUB):
            rows = slice(r, r + ATTN_SUB)
            s = lax.dot_general(q[rows], k, (((1,), (1,)), ((), ())), preferred_element_type=F32)
            m_prev = m_ref[rows, :]
            m_next = jnp.maximum(m_prev, jnp.max(s, axis=-1, keepdims=True))
            alpha = jnp.exp(m_prev - m_next)
            p = jnp.exp(s - jnp.concatenate([m_next] * (size // HEAD_DIM), axis=1))
            acc_ref[rows, :] = jnp.concatenate([alpha, alpha], axis=1) * acc_ref[rows, :] + _dot(p.astype(BF16), vx)
            m_ref[rows, :] = m_next

    chunk(0, first)
    for i in range(n_chunks):
        chunk(first + i * tk, tk)
    acc = acc_ref[...]
    o = acc[:, :HEAD_DIM] * (1.0 / acc[:, HEAD_DIM:])
    for g in range(GQA_GROUP):
        o_ref[0, :, g * HEAD_DIM:(g + 1) * HEAD_DIM] = o[g * tq:(g + 1) * tq].astype(BF16)


def attention(q, kv_segments):
    B, Tq, QW = q.shape
    KVW = kv_segments[0][0].shape[2]
    Tk = sum(k.shape[1] for k, _ in kv_segments)
    nkv = KVW // HEAD_DIM
    tq = _tile(Tq, ATTN_TQ)
    tk = min(ATTN_TK, Tk)
    n_chunks = (Tk - 1) // tk
    first = Tk - n_chunks * tk
    assert first % HEAD_DIM == 0 and tk % HEAD_DIM == 0
    gw = GQA_GROUP * HEAD_DIM
    kern = functools.partial(_attn_kernel, n_seg=len(kv_segments), tq=tq, first=first, tk=tk, n_chunks=n_chunks)
    seg_spec = lambda a: pl.BlockSpec((1, a.shape[1], HEAD_DIM), lambda b, h, i: (b, 0, h))
    return pl.pallas_call(
        kern,
        grid=(B, nkv, Tq // tq),
        in_specs=[pl.BlockSpec((1, tq, gw), lambda b, h, i: (b, i, h))]
        + [seg_spec(a) for kv in kv_segments for a in kv],
        out_specs=pl.BlockSpec((1, tq, gw), lambda b, h, i: (b, i, h)),
        out_shape=jax.ShapeDtypeStruct((B, Tq, QW), BF16),
        scratch_shapes=[
            pltpu.VMEM((Tk, HEAD_DIM), BF16),
            pltpu.VMEM((Tk, 2 * HEAD_DIM), BF16),
            pltpu.VMEM((GQA_GROUP * tq, HEAD_DIM), F32),
            pltpu.VMEM((GQA_GROUP * tq, 2 * HEAD_DIM), F32),
        ],
        compiler_params=_params("arbitrary", "arbitrary", "arbitrary"),
        name="attention",
    )(q, *[a for kv in kv_segments for a in kv])


def _gelu_tanh(x):
    half = 0.5 * x
    return half + half * jnp.tanh(x * (GELU_K0 + (GELU_K0 * GELU_K1) * (x * x)))


def _lru_kernel(xl_ref, gl_ref, xc_ref, gc_ref, cw_ref, cb_ref, wr_ref, br_ref, wi_ref, bi_ref, lam_ref,
                yl_ref, yc_ref, pad_ref, al_ref, bl_ref, hl_ref, ac_ref, bc_ref, hc_ref, *, Ll, Lc):
    S = V7X_SUBLANES
    cw = cw_ref[0]
    cb = cb_ref[0]

    def coeffs(x_ref, L, a_ref, b_ref):
        pad_ref[0:S, :] = jnp.zeros((S, LRU_BLOCK), F32)
        pad_ref[S:S + L, :] = x_ref[0].astype(F32)
        pad_ref[S + L:2 * S + L, :] = jnp.zeros((S, LRU_BLOCK), F32)
        u = cb
        for j in range(CONV_W):
            start = S - CONV_LEFT + j
            u = u + pad_ref[start:start + L, :] * cw[j:j + 1, :]
        ub = u.astype(BF16)
        half_u = 0.5 * u
        for d in range(2):
            tr = jnp.tanh(0.5 * (_dot(ub, wr_ref[0, d, 0]) + br_ref[0, d:d + 1, :]))
            ti = jnp.tanh(0.5 * (_dot(ub, wi_ref[0, d, 0]) + bi_ref[0, d:d + 1, :]))
            nl = -lam_ref[0, d:d + 1, :]
            softplus = jnp.maximum(nl, 0.0) + jnp.log(1.0 + jnp.exp(-jnp.abs(nl)))
            k = (-0.5 * LRU_C * LOG2_E) * softplus
            a = jnp.exp2(k + k * tr)
            a_ref[d] = a
            gap = 1.0 - a * a
            b_ref[d] = gap * lax.rsqrt(jnp.maximum(gap, TINY_F32)) * (half_u + half_u * ti)

    row = lax.broadcasted_iota(jnp.int32, (S, LRU_BLOCK), 0)

    def tile_scan(a, b, carry, reverse):
        for d in (1, 2, 4):
            if reverse:
                keep = row < S - d
                shift = S - d
            else:
                keep = row >= d
                shift = d
            b = b + a * jnp.where(keep, pltpu.roll(b, shift, 0), 0.0)
            a = a * jnp.where(keep, pltpu.roll(a, shift, 0), 1.0)
        h = b + a * carry
        last = h[0:1, :] if reverse else h[S - 1:S, :]
        return h, jnp.broadcast_to(last, (S, LRU_BLOCK))

    def scan(L, a_ref, b_ref, h_ref, carry_f, carry_r):
        n = L // S

        def body(s, carry):
            cf, cr = carry
            for j in range(LRU_SCAN_UNROLL):
                rf = pl.ds(pl.multiple_of((s * LRU_SCAN_UNROLL + j) * S, S), S)
                hf, cf = tile_scan(a_ref[0, rf, :], b_ref[0, rf, :], cf, False)
                h_ref[0, rf, :] = hf
                rr = pl.ds(pl.multiple_of((n - 1 - s * LRU_SCAN_UNROLL - j) * S, S), S)
                hr, cr = tile_scan(a_ref[1, rr, :], b_ref[1, rr, :], cr, True)
                h_ref[1, rr, :] = hr
            return cf, cr

        return lax.fori_loop(0, n // LRU_SCAN_UNROLL, body, (carry_f, carry_r))

    def finish(h_ref, g_ref, y_ref):
        y_ref[0] = ((h_ref[0] + h_ref[1]) * _gelu_tanh(g_ref[0].astype(F32))).astype(BF16)

    coeffs(xc_ref, Lc, ac_ref, bc_ref)
    coeffs(xl_ref, Ll, al_ref, bl_ref)
    zero = jnp.zeros((S, LRU_BLOCK), F32)
    end_f, end_r = scan(Lc, ac_ref, bc_ref, hc_ref, zero, zero)
    scan(Ll, al_ref, bl_ref, hl_ref, end_f, end_r)
    finish(hc_ref, gc_ref, yc_ref)
    finish(hl_ref, gl_ref, yl_ref)


def lru_branch(proj_l, proj_c, lru_w, l, conv_w, conv_b, w_rg, b_rg, w_ig, b_ig, lam):
    B, Ll, _ = proj_l.shape
    Lc = proj_c.shape[1]
    nb = lru_w // LRU_BLOCK
    S = V7X_SUBLANES
    assert Ll % (S * LRU_SCAN_UNROLL) == 0 and Lc % (S * LRU_SCAN_UNROLL) == 0
    blk = lambda L, off: pl.BlockSpec((1, L, LRU_BLOCK), lambda b, h: (b, 0, h + off))
    chan = lambda rows: pl.BlockSpec((1, rows, LRU_BLOCK), lambda b, h: (l, 0, h))
    wspec = pl.BlockSpec((1, 2, 1, LRU_BLOCK, LRU_BLOCK), lambda b, h: (l, 0, h, 0, 0))
    kern = functools.partial(_lru_kernel, Ll=Ll, Lc=Lc)
    return pl.pallas_call(
        kern,
        grid=(B, nb),
        in_specs=[blk(Ll, 0), blk(Ll, nb), blk(Lc, 0), blk(Lc, nb), chan(CONV_W), chan(1),
                  wspec, chan(2), wspec, chan(2), chan(2)],
        out_specs=[pl.BlockSpec((1, Ll, LRU_BLOCK), lambda b, h: (b, 0, h)),
                   pl.BlockSpec((1, Lc, LRU_BLOCK), lambda b, h: (b, 0, h))],
        out_shape=[jax.ShapeDtypeStruct((B, Ll, lru_w), BF16), jax.ShapeDtypeStruct((B, Lc, lru_w), BF16)],
        scratch_shapes=[pltpu.VMEM((Ll + 2 * S, LRU_BLOCK), F32)]
        + [pltpu.VMEM((2, Ll, LRU_BLOCK), F32)] * 3 + [pltpu.VMEM((2, Lc, LRU_BLOCK), F32)] * 3,
        compiler_params=_params("arbitrary", "arbitrary"),
        name="rglru",
    )(proj_l, proj_l, proj_c, proj_c, conv_w, conv_b.reshape(conv_b.shape[0], 1, lru_w), w_rg, b_rg, w_ig, b_ig, lam)


def _merge_kernel(att_ref, lru_ref, ga_ref, gl_ref, wa_ref, wl_ref, o_ref):
    ya = _dot(att_ref[...], wa_ref[0, 0])
    yl = _dot(lru_ref[...], wl_ref[0, 0])
    m = _sigmoid(ga_ref[...].astype(F32)) * ya + _sigmoid(gl_ref[...].astype(F32)) * yl
    o_ref[...] = m.astype(BF16)


def merge_branches(att, lru, proj, ga_off, D, w_branch, l):
    M, K = att.shape
    tm = _tile(M, MATMUL_TILE)
    tn = MATMUL_TILE
    while ga_off % tn or D % tn:
        tn //= 2
    ga_b, gl_b = ga_off // tn, (ga_off + D) // tn
    return pl.pallas_call(
        _merge_kernel,
        grid=(M // tm, D // tn),
        in_specs=[
            pl.BlockSpec((tm, K), lambda i, j: (i, 0)),
            pl.BlockSpec((tm, K), lambda i, j: (i, 0)),
            pl.BlockSpec((tm, tn), lambda i, j: (i, ga_b + j)),
            pl.BlockSpec((tm, tn), lambda i, j: (i, gl_b + j)),
            pl.BlockSpec((1, 1, K, tn), lambda i, j: (l, 0, 0, j)),
            pl.BlockSpec((1, 1, K, tn), lambda i, j: (l, 1, 0, j)),
        ],
        out_specs=pl.BlockSpec((tm, tn), lambda i, j: (i, j)),
        out_shape=jax.ShapeDtypeStruct((M, D), BF16),
        compiler_params=_params("arbitrary", "arbitrary"),
        name="merge",
    )(att, lru, proj, proj, w_branch, w_branch)


def _postln_kernel(*refs, gate_row, sh_row, sc_row, alpha, router, emit_h):
    x_ref, y_ref, mod_ref, modn_ref, lng_ref, lnb_ref = refs[:6]
    rest = list(refs[6:])
    wr_ref = rest.pop(0) if router else None
    xo_ref = rest.pop(0)
    ho_ref = rest.pop(0) if emit_h else None
    lg_ref = rest.pop(0) if router else None
    z = alpha * x_ref[0] + mod_ref[0, gate_row:gate_row + 1, :] * y_ref[0].astype(F32)
    mu = jnp.mean(z, axis=-1, keepdims=True)
    zc = z - mu
    var = jnp.mean(zc * zc, axis=-1, keepdims=True)
    xn = zc * lax.rsqrt(var + NORM_EPS) * lng_ref[...] + lnb_ref[...]
    xo_ref[0] = xn
    if emit_h:
        h = xn * (1.0 + modn_ref[0, sc_row:sc_row + 1, :]) + modn_ref[0, sh_row:sh_row + 1, :]
        h_hi = h.astype(BF16)
        ho_ref[0] = h_hi
        if router:
            h_lo = (h - h_hi.astype(F32)).astype(BF16)
            nt = lambda a, b: lax.dot_general(a, b, (((1,), (1,)), ((), ())), preferred_element_type=F32)
            lg_ref[0] = nt(wr_ref[0], h_hi) + (nt(wr_ref[0], h_lo) + nt(wr_ref[1], h_hi))


def post_ln(x, y, mod, modn, ln_g, ln_b, alpha, gate_row, sh_row, sc_row, w_router_t=None, emit_h=True):
    B, T, D = x.shape
    tr = _tile(T, LN_ROW_TILE)
    router = w_router_t is not None
    mspec = lambda m: pl.BlockSpec((1, 6, D), (lambda b, i: (b, 0, 0)) if m.shape[0] > 1 else (lambda b, i: (0, 0, 0)))
    row = pl.BlockSpec((1, tr, D), lambda b, i: (b, i, 0))
    vec = pl.BlockSpec((1, D), lambda b, i: (0, 0))
    in_specs = [row, row, mspec(mod), mspec(modn), vec, vec]
    args = [x, y, mod, modn, ln_g.reshape(1, D), ln_b.reshape(1, D)]
    out_specs = [row]
    out_shape = [jax.ShapeDtypeStruct((B, T, D), F32)]
    if router:
        E = w_router_t.shape[1]
        in_specs.append(pl.BlockSpec((2, E, D), lambda b, i: (0, 0, 0)))
        args.append(w_router_t)
    if emit_h:
        out_specs.append(row)
        out_shape.append(jax.ShapeDtypeStruct((B, T, D), BF16))
    if router:
        out_specs.append(pl.BlockSpec((1, E, tr), lambda b, i: (b, 0, i)))
        out_shape.append(jax.ShapeDtypeStruct((B, E, T), F32))
    kern = functools.partial(_postln_kernel, gate_row=gate_row, sh_row=sh_row, sc_row=sc_row, alpha=alpha,
                             router=router, emit_h=emit_h)
    return pl.pallas_call(
        kern,
        grid=(B, T // tr),
        in_specs=in_specs,
        out_specs=out_specs,
        out_shape=out_shape,
        compiler_params=_params("arbitrary", "arbitrary"),
        name="post_ln",
    )(*args)


def _topk_kernel(lg_ref, pos_ref, gate_ref, cnt_ref, *, T, cap, tw):
    lg = lg_ref[0]
    E = lg.shape[0]
    ex = jnp.exp(lg - jnp.max(lg, axis=0, keepdims=True))
    aff = ex / jnp.sum(ex, axis=0, keepdims=True)
    bits = lax.bitcast_convert_type(aff, jnp.int32)

    def count(mask):
        return jnp.sum(jnp.where(mask, 1.0, 0.0), axis=1, keepdims=True)

    thr = jnp.zeros((E, 1), jnp.int32)
    for bit in range(30, -1, -1):
        cand = thr | (1 << bit)
        thr = jnp.where(count(bits >= cand) >= cap, cand, thr)

    t_row = lax.broadcasted_iota(jnp.int32, (T, V7X_LANES), 0)
    w_col = lax.broadcasted_iota(jnp.int32, (T, V7X_LANES), 1)
    before_window = jnp.where(t_row < w_col * tw, 1.0, 0.0).astype(BF16)
    r_i = lax.broadcasted_iota(jnp.int32, (tw, tw), 0)
    c_i = lax.broadcasted_iota(jnp.int32, (tw, tw), 1)
    strict_upper = jnp.where(r_i < c_i, 1.0, 0.0).astype(BF16)

    def prefix(mask):
        mb = jnp.where(mask, 1.0, 0.0).astype(BF16)
        starts = _dot(mb, before_window)
        parts = [_dot(mb[:, w * tw:(w + 1) * tw], strict_upper) + starts[:, w:w + 1] for w in range(T // tw)]
        return jnp.concatenate(parts, axis=1) if len(parts) > 1 else parts[0], starts

    gt = bits > thr
    eq = bits == thr
    need = cap - count(gt)
    tie_rank, _ = prefix(eq)
    sel = gt | (eq & (tie_rank < need))
    slot, starts = prefix(sel)
    pos_ref[0] = jnp.where(sel, slot.astype(jnp.int32), -1)
    gate_ref[0] = jnp.where(sel, aff, 0.0)
    cnt_ref[0] = starts.astype(jnp.int32)


def expert_topk(logits, cap):
    B, E, T = logits.shape
    tw = min(TOKEN_WINDOW, T)
    kern = functools.partial(_topk_kernel, T=T, cap=cap, tw=tw)
    spec = pl.BlockSpec((1, E, T), lambda b: (b, 0, 0))
    return pl.pallas_call(
        kern,
        grid=(B,),
        in_specs=[spec],
        out_specs=[spec, spec, pl.BlockSpec((1, E, V7X_LANES), lambda b: (b, 0, 0))],
        out_shape=[jax.ShapeDtypeStruct((B, E, T), jnp.int32), jax.ShapeDtypeStruct((B, E, T), F32),
                   jax.ShapeDtypeStruct((B, E, V7X_LANES), jnp.int32)],
        compiler_params=_params("arbitrary"),
        name="expert_topk",
    )(logits)


def _slot_span(cnt_ref, idx, C, sp):
    lo = cnt_ref[idx]
    hi = cnt_ref[idx + 1]
    start = jnp.minimum((lo // BF16_ROWS) * BF16_ROWS, C - sp)
    n_extra = jnp.maximum(hi - start - 1, 0) // sp
    return pl.multiple_of(start, BF16_ROWS), n_extra


def _gather_kernel(cnt_ref, pos_ref, h_ref, xs_ref, *, C, sp, nw, E):
    b = pl.program_id(0)
    w = pl.program_id(2)

    @pl.when(w == 0)
    def _():
        xs_ref[...] = jnp.zeros(xs_ref.shape, BF16)

    hwin = h_ref[0]
    tw = hwin.shape[0]
    slot_i = lax.broadcasted_iota(jnp.int32, (sp, tw), 0)
    spans = [_slot_span(cnt_ref, (b * E + e) * (nw + 1) + w, C, sp) for e in range(E)]
    onehots = [jnp.where(pos_ref[0, e:e + 1, :] == slot_i + spans[e][0], 1.0, 0.0).astype(BF16)
               for e in range(E)]
    rows = _dot(jnp.concatenate(onehots, axis=0), hwin)
    for e in range(E):
        start, n_extra = spans[e]
        xs_ref[0, e, pl.ds(start, sp), :] += rows[e * sp:(e + 1) * sp].astype(BF16)

    @pl.when(sum(n for _, n in spans) > 0)
    def _():
        for e in range(E):
            start, n_extra = spans[e]

            def extra(i, carry):
                j0 = start + sp * (i + 1)
                j0c = pl.multiple_of(jnp.minimum(j0, C - sp), BF16_ROWS)
                slot = slot_i + j0c
                hit = (pos_ref[0, e:e + 1, :] == slot) & (slot >= j0)
                xs_ref[0, e, pl.ds(j0c, sp), :] += _dot(jnp.where(hit, 1.0, 0.0).astype(BF16), h_ref[0]).astype(BF16)
                return carry

            lax.fori_loop(0, n_extra, extra, 0)


def gather_tokens(h, pos, cnt_flat, C):
    B, T, D = h.shape
    E = pos.shape[1]
    tw = min(TOKEN_WINDOW, T)
    nw = T // tw
    sp = min(SLOT_SPAN, C)
    dc = _tile(D, MOE_COL_TILE)
    kern = functools.partial(_gather_kernel, C=C, sp=sp, nw=nw, E=E)
    grid_spec = pltpu.PrefetchScalarGridSpec(
        num_scalar_prefetch=1,
        grid=(B, D // dc, nw),
        in_specs=[
            pl.BlockSpec((1, E, tw), lambda b, d, w, cnt: (b, 0, w)),
            pl.BlockSpec((1, tw, dc), lambda b, d, w, cnt: (b, w, d)),
        ],
        out_specs=pl.BlockSpec((1, E, C, dc), lambda b, d, w, cnt: (b, 0, 0, d)),
    )
    return pl.pallas_call(
        kern,
        grid_spec=grid_spec,
        out_shape=jax.ShapeDtypeStruct((B, E, C, D), BF16),
        compiler_params=_params("arbitrary", "arbitrary", "arbitrary"),
        name="moe_gather",
    )(cnt_flat, pos, h)


def _slot_gates(pos_ref, gate_ref, C):
    T = pos_ref.shape[-1]
    rows = min(C, FFN_GATE_ROWS)
    cols = []
    for r0 in range(0, C, rows):
        slot = lax.broadcasted_iota(jnp.int32, (rows, V7X_LANES), 0) + r0
        acc = jnp.zeros((rows, V7X_LANES), F32)
        for t0 in range(0, T, V7X_LANES):
            p = pos_ref[0, 0, :, t0:t0 + V7X_LANES]
            acc = acc + jnp.where(p == slot, gate_ref[0, 0, :, t0:t0 + V7X_LANES], 0.0)
        cols.append(jnp.sum(acc, axis=-1, keepdims=True))
    return cols[0] if len(cols) == 1 else jnp.concatenate(cols, axis=0)


def _ffn_kernel(*refs, F, n):
    ins, (wgu_ref, wd_ref), y_refs = refs[:3 * n], refs[3 * n:3 * n + 2], refs[3 * n + 2:]
    xs_refs, pos_refs, gate_refs = ins[0::3], ins[1::3], ins[2::3]
    x = xs_refs[0][0, 0] if n == 1 else jnp.concatenate([r[0, 0] for r in xs_refs], axis=0)
    gates = [_slot_gates(p, g, r.shape[2]) for r, p, g in zip(xs_refs, pos_refs, gate_refs)]
    gu = _dot(x, wgu_ref[0, 0])
    g = gu[:, :F]
    u = gu[:, F:]
    act = (g * _sigmoid(g) * u).astype(BF16)
    y = _dot(act, wd_ref[0, 0])
    row = 0
    for r, gate in zip(y_refs, gates):
        rows = r.shape[2]
        r[0, 0] = (y[row:row + rows] * gate).astype(BF16)
        row += rows


def expert_ffn(streams, w_gate_up, w_down, l):
    B, E, _, D = streams[0][0].shape
    F = w_down.shape[2]
    slot_spec = lambda xs: pl.BlockSpec((1, 1, xs.shape[2], D), lambda e, b: (b, e, 0, 0))
    tok_spec = lambda T: pl.BlockSpec((1, 1, 1, T), lambda e, b: (b, e, 0, 0))
    in_specs, args = [], []
    for xs, pos, gate in streams:
        T = pos.shape[2]
        in_specs += [slot_spec(xs), tok_spec(T), tok_spec(T)]
        args += [xs, pos.reshape(B, E, 1, T), gate.reshape(B, E, 1, T)]
    kern = functools.partial(_ffn_kernel, F=F, n=len(streams))
    return pl.pallas_call(
        kern,
        grid=(E, B),
        in_specs=in_specs + [
            pl.BlockSpec((1, 1, D, 2 * F), lambda e, b: (l, e, 0, 0)),
            pl.BlockSpec((1, 1, F, D), lambda e, b: (l, e, 0, 0)),
        ],
        out_specs=[slot_spec(xs) for xs, _, _ in streams],
        out_shape=[jax.ShapeDtypeStruct(xs.shape, BF16) for xs, _, _ in streams],
        compiler_params=_params("arbitrary", "arbitrary"),
        name="expert_ffn",
    )(*args, w_gate_up, w_down)


def _scatter_kernel(cnt_ref, post_ref, y_ref, o_ref, acc_ref, *, C, sp, nw, E):
    b = pl.program_id(0)
    w = pl.program_id(2)
    tw = post_ref.shape[2]
    slot_i = lax.broadcasted_iota(jnp.int32, (sp, tw), 0)

    def onehot(e, j0c, j0):
        slot = slot_i + j0c
        hit = post_ref[0, e:e + 1, :] == slot
        if j0 is not None:
            hit = hit & (slot >= j0)
        return jnp.where(hit, 1.0, 0.0).astype(BF16)

    def contribution(experts, starts, j0):
        lhs = [onehot(e, s, j0) for e, s in zip(experts, starts)]
        rhs = [y_ref[0, e, pl.ds(s, sp), :] for e, s in zip(experts, starts)]
        if len(lhs) > 1:
            lhs, rhs = [jnp.concatenate(lhs, axis=0)], [jnp.concatenate(rhs, axis=0)]
        return lax.dot_general(lhs[0], rhs[0], (((0,), (0,)), ((), ())), preferred_element_type=F32)

    spans = [_slot_span(cnt_ref, (b * E + e) * (nw + 1) + w, C, sp) for e in range(E)]
    group = max(1, min(E, V7X_MXU_DEPTH // sp))
    total = None
    for e0 in range(0, E, group):
        es = tuple(range(e0, min(e0 + group, E)))
        r = contribution(es, tuple(spans[e][0] for e in es), None)
        total = r if total is None else total + r
    o_ref[0] = total.astype(o_ref.dtype)

    @pl.when(sum(n for _, n in spans) > 0)
    def _():
        acc_ref[...] = total
        for e in range(E):
            start, n_extra = spans[e]

            def extra(i, carry):
                j0 = start + sp * (i + 1)
                j0c = pl.multiple_of(jnp.minimum(j0, C - sp), BF16_ROWS)
                acc_ref[...] += contribution((e,), (j0c,), j0)
                return carry

            lax.fori_loop(0, n_extra, extra, 0)
        o_ref[0] = acc_ref[...].astype(o_ref.dtype)


def scatter_tokens(y, pos_t, cnt_flat):
    B, E, C, D = y.shape
    T = pos_t.shape[2]
    tw = min(TOKEN_WINDOW, T)
    nw = T // tw
    sp = min(SLOT_SPAN, C)
    dc = _tile(D, MOE_COL_TILE)
    kern = functools.partial(_scatter_kernel, C=C, sp=sp, nw=nw, E=E)
    grid_spec = pltpu.PrefetchScalarGridSpec(
        num_scalar_prefetch=1,
        grid=(B, D // dc, nw),
        in_specs=[
            pl.BlockSpec((1, E, tw), lambda b, d, w, cnt: (b, 0, w)),
            pl.BlockSpec((1, E, C, dc), lambda b, d, w, cnt: (b, 0, 0, d)),
        ],
        out_specs=pl.BlockSpec((1, tw, dc), lambda b, d, w, cnt: (b, w, d)),
        scratch_shapes=[pltpu.VMEM((tw, dc), F32)],
    )
    return pl.pallas_call(
        kern,
        grid_spec=grid_spec,
        out_shape=jax.ShapeDtypeStruct((B, T, D), BF16),
        compiler_params=_params("arbitrary", "arbitrary", "arbitrary"),
        name="moe_scatter",
    )(cnt_flat, pos_t, y)


def expert_choice_ffn(streams, w_gate_up, w_down, l):
    routed = []
    for h, logits in streams:
        T, E = h.shape[1], logits.shape[1]
        cap = CAP_FACTOR * T // E
        nw = T // min(TOKEN_WINDOW, T)
        pos, gate, cnt = expert_topk(logits, cap)
        cnt_flat = cnt[:, :, :nw + 1].reshape(-1)
        routed.append((gather_tokens(h, pos, cnt_flat, cap), pos, gate, cnt_flat))
    ys = expert_ffn([(xs, pos, gate) for xs, pos, gate, _ in routed], w_gate_up, w_down, l)
    return [scatter_tokens(y, pos, cnt_flat) for y, (_, pos, _, cnt_flat) in zip(ys, routed)]


def _rope_tables(T):
    rows = T // GRID_W
    row = jnp.repeat(jnp.arange(rows, dtype=F32), GRID_W)
    col = jnp.tile(jnp.arange(GRID_W, dtype=F32), rows)
    n_freq = HEAD_DIM // 4
    inv = ROPE_THETA ** (-jnp.arange(n_freq, dtype=F32) / n_freq)
    ang = jnp.concatenate([row[:, None] * inv, col[:, None] * inv], axis=-1)
    cos2 = jnp.repeat(jnp.cos(ang), 2, axis=-1)
    sin = jnp.sin(ang)
    sin2 = jnp.stack([-sin, sin], axis=-1).reshape(T, HEAD_DIM)
    return cos2, sin2


def kernel(x, c, ctx, c_ctx, w_ada_dn, w_ada_up, b_ada, w_in, q_norm_g, k_norm_g, conv_w, conv_b, w_rg, b_rg, w_ig, b_ig, lru_lambda, w_branch, w_out, ln_g, ln_b, w_router, w_gate_up, w_down):
    B, T, D = x.shape
    Tc = ctx.shape[1]
    depth = w_in.shape[0]
    in_w = w_in.shape[2]
    lru_w = conv_w.shape[2]
    kvw = (in_w - 2 * lru_w - 2 * D) // (GQA_GROUP + 2)
    qw = GQA_GROUP * kvw
    q_off = 2 * lru_w
    ga_off = q_off + qw + 2 * kvw
    alpha = (2.0 * depth) ** 0.25

    n_rows = -(-(B + 1) // V7X_SUBLANES) * V7X_SUBLANES
    cv = jnp.zeros((n_rows, D), F32).at[:B].set(c).at[B].set(c_ctx)
    mods = ada_modulation(cv, w_ada_dn, w_ada_up, b_ada).reshape(depth, n_rows, 6, D)
    mod_l = [mods[l, :B] for l in range(depth)]
    mod_c = [mods[l, B:B + 1] for l in range(depth)]

    cos2, sin2 = _rope_tables(T)
    w_in_b = w_in.astype(BF16)
    w_branch_b = w_branch.astype(BF16)
    w_out_b = w_out.astype(BF16)
    w_rg_b = w_rg.astype(BF16)
    w_ig_b = w_ig.astype(BF16)
    w_gu_b = w_gate_up.astype(BF16)
    w_dn_b = w_down.astype(BF16)
    w_router_f = jnp.swapaxes(w_router, 1, 2)
    w_router_hi = w_router_f.astype(BF16)
    w_router_t = jnp.stack([w_router_hi, (w_router_f - w_router_hi.astype(F32)).astype(BF16)], axis=1)

    xl, xc = x, ctx
    hl = modulate_rows(xl, mod_l[0])
    hc = modulate_rows(xc, mod_c[0])
    for l in range(depth):
        last = l == depth - 1
        proj_l = matmul(hl.reshape(B * T, D), w_in_b, l).reshape(B, T, in_w)
        proj_c = matmul(hc.reshape(B * Tc, D), w_in_b, l).reshape(B, Tc, in_w)
        q_l, k_l, v_l = norm_rope(proj_l, q_off, qw, kvw, q_norm_g[l], k_norm_g[l], cos2, sin2)
        q_c, k_c, v_c = norm_rope(proj_c, q_off, qw, kvw, q_norm_g[l], k_norm_g[l], None, None)
        att_l = attention(q_l, [(k_c, v_c), (k_l, v_l)])
        lru_l, lru_c = lru_branch(proj_l, proj_c, lru_w, l, conv_w, conv_b, w_rg_b, b_rg, w_ig_b, b_ig, lru_lambda)

        def mixer_tail(xs, hs_proj, att, lru, mod, Ts):
            m = merge_branches(att.reshape(B * Ts, qw), lru.reshape(B * Ts, lru_w), hs_proj.reshape(B * Ts, in_w),
                               ga_off, D, w_branch_b, l)
            mix = matmul(m, w_out_b, l).reshape(B, Ts, D)
            return post_ln(xs, mix, mod, mod, ln_g[l, 0], ln_b[l, 0], alpha, 2, 3, 4, w_router_t[l])

        xl1, hl2, logits_l = mixer_tail(xl, proj_l, att_l, lru_l, mod_l[l], T)
        if last:
            (moe_l,) = expert_choice_ffn([(hl2, logits_l)], w_gu_b, w_dn_b, l)
            (xl,) = post_ln(xl1, moe_l, mod_l[l], mod_l[l], ln_g[l, 1], ln_b[l, 1], alpha, 5, 0, 1, emit_h=False)
        else:
            att_c = attention(q_c, [(k_c, v_c)])
            xc1, hc2, logits_c = mixer_tail(xc, proj_c, att_c, lru_c, mod_c[l], Tc)
            moe_l, moe_c = expert_choice_ffn([(hl2, logits_l), (hc2, logits_c)], w_gu_b, w_dn_b, l)
            xl, hl = post_ln(xl1, moe_l, mod_l[l], mod_l[l + 1], ln_g[l, 1], ln_b[l, 1], alpha, 5, 0, 1)
            xc, hc = post_ln(xc1, moe_c, mod_c[l], mod_c[l + 1], ln_g[l, 1], ln_b[l, 1], alpha, 5, 0, 1)
    return xl
```
